```python
import math
import jax, jax.numpy as jnp
from jax import lax
import numpy as np

D_MODEL = 2048
BATCH = 2
SEQ = 16384
DEPTH = 2

CHUNK = 64
N_MEM = 256
EPS = 1e-6
POOL_W = D_MODEL // 2
POOL_WINDOWS = (2, 4, 8, 16)
POOL_GROUPS = len(POOL_WINDOWS)
POOL_GW = POOL_W // POOL_GROUPS
CONV_W = D_MODEL // 2
CONV_K = 3
IN_SPLITS = (POOL_W, POOL_W + CONV_W, POOL_W + 2 * CONV_W, POOL_W + 3 * CONV_W,
             POOL_W + 3 * CONV_W + D_MODEL)
IN_COLS = POOL_W + 3 * CONV_W + 2 * D_MODEL
XA_HEADS = 4
XA_HEAD_DIM = D_MODEL // XA_HEADS
N_GROUPS = 4
EXPERTS_PER_GROUP = 8
N_EXPERTS = N_GROUPS * EXPERTS_PER_GROUP
TOP_K = 2
D_EXPERT = D_MODEL // 2
ROW_BLOCK = 128

kernel_name = "hybrid_pool_shortconv_memxattn_hiermoe"


def rms_norm(x, g):
    xf = x.astype(jnp.float32)
    y = xf * lax.rsqrt(jnp.mean(xf * xf, axis=-1, keepdims=True) + EPS)
    return (y * g.astype(jnp.float32)).astype(x.dtype)


def pool_branch(u, pool_mix, pool_scale):
    bsz, s, _ = u.shape
    uf = u.astype(jnp.float32)
    cs = jnp.cumsum(uf, axis=1)
    pos = jnp.arange(1, s + 1, dtype=jnp.int32)
    outs = []
    for gi, w in enumerate(POOL_WINDOWS):
        c = cs[..., gi * POOL_GW:(gi + 1) * POOL_GW]
        lag = jnp.pad(c, ((0, 0), (w, 0), (0, 0)))[:, :s]
        cnt = jnp.minimum(pos, w).astype(jnp.float32)[None, :, None]
        outs.append((c - lag) / cnt - uf[..., gi * POOL_GW:(gi + 1) * POOL_GW])
    p = jnp.stack(outs, axis=2).astype(u.dtype)
    p = jnp.einsum('bsgc,gcd->bsgd', p, pool_mix).reshape(bsz, s, POOL_W)
    return p * pool_scale


def short_conv_branch(u, b_gate, c_gate, conv_w):
    v = c_gate * u
    s = v.shape[1]
    vp = jnp.pad(v, ((0, 0), (CONV_K - 1, 0), (0, 0)))
    y = conv_w[CONV_K - 1] * v
    for j in range(CONV_K - 1):
        y = y + conv_w[j] * vp[:, j:j + s]
    return b_gate * y


def memory_cross_attention(h, mem_n, w_q, w_k, w_v, w_o):
    bsz, s, _ = h.shape
    q = (h @ w_q).reshape(bsz, s, XA_HEADS, XA_HEAD_DIM)
    k = (mem_n @ w_k).reshape(bsz, N_MEM, XA_HEADS, XA_HEAD_DIM)
    v = (mem_n @ w_v).reshape(bsz, N_MEM, XA_HEADS, XA_HEAD_DIM)
    scores = jnp.einsum('bshd,bmhd->bhsm', q, k).astype(jnp.float32) / math.sqrt(XA_HEAD_DIM)
    probs = jax.nn.softmax(scores, axis=-1).astype(h.dtype)
    o = jnp.einsum('bhsm,bmhd->bshd', probs, v).reshape(bsz, s, D_MODEL)
    return o @ w_o


def hierarchical_moe(h, w_rg, b_rg, w_re, b_re, w_gate, w_up, w_down):
    bsz, s, d = h.shape
    n = bsz * s
    hf = h.reshape(n, d)
    g_logits = (hf @ w_rg).astype(jnp.float32) + b_rg.astype(jnp.float32)
    g_sel = jnp.argmax(g_logits, axis=-1).astype(jnp.int32)
    p_group = jnp.take_along_axis(jax.nn.softmax(g_logits, axis=-1), g_sel[:, None], axis=1)
    e_logits = ((hf @ w_re).astype(jnp.float32) + b_re.astype(jnp.float32)).reshape(
        n, N_GROUPS, EXPERTS_PER_GROUP)
    e_in_group = jnp.take_along_axis(e_logits, g_sel[:, None, None], axis=1)[:, 0]
    top_v, top_i = lax.top_k(e_in_group, TOP_K)
    gates = (p_group * jax.nn.softmax(top_v, axis=-1)).astype(h.dtype)
    expert_ids = g_sel[:, None] * EXPERTS_PER_GROUP + top_i.astype(jnp.int32)

    a = n * TOP_K
    e_flat = expert_ids.reshape(a)
    tok_flat = jnp.repeat(jnp.arange(n, dtype=jnp.int32), TOP_K)
    g_flat = gates.reshape(a)
    order = jnp.argsort(e_flat)
    e_s, tok_s, g_s = e_flat[order], tok_flat[order], g_flat[order]
    counts = jnp.zeros((N_EXPERTS,), jnp.int32).at[e_flat].add(1)
    starts = jnp.cumsum(counts) - counts
    padded = (counts + ROW_BLOCK - 1) // ROW_BLOCK * ROW_BLOCK
    pends = jnp.cumsum(padded)
    pstarts = pends - padded
    dest = pstarts[e_s] + (jnp.arange(a, dtype=jnp.int32) - starts[e_s])
    rows = a + N_EXPERTS * ROW_BLOCK
    n_blocks = rows // ROW_BLOCK
    x_pad = jnp.zeros((rows, d), h.dtype).at[dest].set(hf[tok_s])
    block_start = jnp.arange(n_blocks, dtype=jnp.int32) * ROW_BLOCK
    block_e = jnp.clip(jnp.searchsorted(pends, block_start, side='right'), 0, N_EXPERTS - 1).astype(jnp.int32)

    def expert_block(args):
        xb, e = args
        hid = jax.nn.silu(xb @ w_gate[e]) * (xb @ w_up[e])
        return hid @ w_down[e]

    y_pad = lax.map(expert_block, (x_pad.reshape(n_blocks, ROW_BLOCK, d), block_e)).reshape(rows, d)
    y = y_pad[dest] * g_s[:, None]
    out = jnp.zeros((n, d), h.dtype).at[tok_s].add(y)
    return out.reshape(bsz, s, d)


def setup_inputs(seed: int = 0) -> dict:
    key = jax.random.key(seed)
    ks = iter(jax.random.split(key, 32))
    f32 = jnp.float32

    def nrm(shape, scale):
        return jax.random.normal(next(ks), shape, f32) * scale

    def gain(shape):
        return 1.0 + 0.05 * jax.random.normal(next(ks), shape, f32)

    L, D = DEPTH, D_MODEL
    return {
        "x": nrm((BATCH, SEQ, D), 1.0),
        "mem": nrm((BATCH, N_MEM, D), 1.0),
        "w_in": nrm((L, D, IN_COLS), D ** -0.5),
        "pool_mix": nrm((L, POOL_GROUPS, POOL_GW, POOL_GW), POOL_GW ** -0.5),
        "pool_scale": gain((L, POOL_W)),
        "w_pool_out": nrm((L, POOL_W, D), POOL_W ** -0.5),
        "conv_w": nrm((L, CONV_K, CONV_W), CONV_K ** -0.5),
        "w_conv_out": nrm((L, CONV_W, D), CONV_W ** -0.5),
        "w_mix_out": nrm((L, D, D), D ** -0.5),
        "g_mix": gain((L, D)),
        "g_xattn": gain((L, D)),
        "g_mem": gain((L, D)),
        "w_q": nrm((L, D, D), D ** -0.5),
        "w_k": nrm((L, D, D), D ** -0.5),
        "w_v": nrm((L, D, D), D ** -0.5),
        "w_o": nrm((L, D, D), D ** -0.5),
        "g_ffn": gain((L, D)),
        "w_route_group": nrm((L, D, N_GROUPS), D ** -0.5),
        "b_route_group": nrm((L, N_GROUPS), 0.01),
        "w_route_expert": nrm((L, D, N_EXPERTS), D ** -0.5),
        "b_route_expert": nrm((L, N_EXPERTS), 0.01),
        "w_gate": nrm((L, N_EXPERTS, D, D_EXPERT), D ** -0.5),
        "w_up": nrm((L, N_EXPERTS, D, D_EXPERT), D ** -0.5),
        "w_down": nrm((L, N_EXPERTS, D_EXPERT, D), D_EXPERT ** -0.5),
        "g_final": gain((D,)),
    }


def reference(x, mem, w_in, pool_mix, pool_scale, w_pool_out, conv_w, w_conv_out, w_mix_out,
              g_mix, g_xattn, g_mem, w_q, w_k, w_v, w_o, g_ffn,
              w_route_group, b_route_group, w_route_expert, b_route_expert,
              w_gate, w_up, w_down, g_final):
    for l in range(DEPTH):
        h = rms_norm(x, g_mix[l])
        z = h @ w_in[l]
        u_pool, u_conv, b_gate, c_gate, gate_a, gate_b = jnp.split(z, IN_SPLITS, axis=-1)
        br_a = pool_branch(u_pool, pool_mix[l], pool_scale[l]) @ w_pool_out[l]
        br_b = short_conv_branch(u_conv, b_gate, c_gate, conv_w[l]) @ w_conv_out[l]
        merged = jax.nn.sigmoid(gate_a) * br_a + jax.nn.sigmoid(gate_b) * br_b
        x = x + merged @ w_mix_out[l]
        h = rms_norm(x, g_xattn[l])
        mem_n = rms_norm(mem, g_mem[l])
        x = x + memory_cross_attention(h, mem_n, w_q[l], w_k[l], w_v[l], w_o[l])
        h = rms_norm(x, g_ffn[l])
        x = x + hierarchical_moe(h, w_route_group[l], b_route_group[l], w_route_expert[l],
                                 b_route_expert[l], w_gate[l], w_up[l], w_down[l])
    return rms_norm(x, g_final)
```

```python
import functools
import math

import jax
import jax.numpy as jnp
from jax import lax
from jax.experimental import pallas as pl
from jax.experimental.pallas import tpu as pltpu

EPS = 1e-6
POOL_WINDOWS = (2, 4, 8, 16)
CONV_K = 3
XA_HEADS = 4
N_GROUPS = 4
EXPERTS_PER_GROUP = 8
N_EXPERTS = N_GROUPS * EXPERTS_PER_GROUP
HALO = 16
ROUTE_LANES = 128
VMEM_LIMIT = 56 * 1024 * 1024

F32 = jnp.float32
BF16 = jnp.bfloat16


def _params(*sem):
    return pltpu.CompilerParams(dimension_semantics=sem, vmem_limit_bytes=VMEM_LIMIT)


def _resident(shape):
    nd = len(shape)
    return pl.BlockSpec(shape, lambda *_: (0,) * nd, pipeline_mode=pl.Buffered(1))


def _rms(xf, g):
    ms = jnp.mean(xf * xf, axis=-1, keepdims=True)
    return xf * lax.rsqrt(ms + EPS) * g


def _dot(a, b):
    return jnp.dot(a, b, preferred_element_type=F32)


def _mixer_in_kernel(x_ref, g_ref, w_ref, z_ref, h_scr):
    @pl.when(pl.program_id(1) == 0)
    def _():
        h_scr[...] = _rms(x_ref[...], g_ref[...]).astype(BF16)

    z_ref[...] = _dot(h_scr[...], w_ref[...]).astype(z_ref.dtype)


def _mixer_in(x, g, w, tm, tn):
    n, d = x.shape
    c = w.shape[1]
    return pl.pallas_call(
        _mixer_in_kernel,
        grid=(n // tm, c // tn),
        in_specs=[
            pl.BlockSpec((tm, d), lambda i, j: (i, 0)),
            pl.BlockSpec((1, d), lambda i, j: (0, 0)),
            pl.BlockSpec((d, tn), lambda i, j: (0, j)),
        ],
        out_specs=pl.BlockSpec((tm, tn), lambda i, j: (i, j)),
        out_shape=jax.ShapeDtypeStruct((n, c), BF16),
        scratch_shapes=[pltpu.VMEM((tm, d), BF16)],
        compiler_params=_params("arbitrary", "arbitrary"),
        name="mixer_in",
    )(x, g, w)


def _mixer_mid_kernel(x_ref, z_ref, zh_ref, pm_ref, ps_ref, wpo_ref, cw_ref, wco_ref, wmo_ref, o_ref,
                      *, tiles_per_seq, tm, d):
    pw = d // 2
    gw = pw // len(POOL_WINDOWS)
    t = pl.program_id(0) % tiles_per_seq
    keep = jnp.where(t == 0, 0.0, 1.0).astype(F32)
    pos = lax.broadcasted_iota(jnp.int32, (tm, gw), 0) + (t * tm + 1)

    parts = []
    for gi, w in enumerate(POOL_WINDOWS):
        cols = slice(gi * gw, (gi + 1) * gw)
        u = z_ref[:, cols].astype(F32)
        ext = jnp.concatenate([zh_ref[:, cols].astype(F32) * keep, u], axis=0)
        s = 1
        while s < w:
            ext = ext + pltpu.roll(ext, s, axis=0)
            s *= 2
        cnt = jnp.minimum(pos, w).astype(F32)
        p = (ext[HALO:, :] / cnt - u).astype(BF16)
        m = _dot(p, pm_ref[gi]) * ps_ref[:, cols]
        parts.append(m.astype(BF16))
    br_a = _dot(jnp.concatenate(parts, axis=1), wpo_ref[...])

    uc = z_ref[:, pw:2 * pw].astype(F32)
    bg = z_ref[:, 2 * pw:3 * pw].astype(F32)
    cg = z_ref[:, 3 * pw:4 * pw].astype(F32)
    v = cg * uc
    vh = zh_ref[:, 3 * pw:4 * pw].astype(F32) * zh_ref[:, pw:2 * pw].astype(F32) * keep
    vext = jnp.concatenate([vh, v], axis=0)
    y = cw_ref[CONV_K - 1:CONV_K, :] * v
    for j in range(CONV_K - 1):
        y = y + cw_ref[j:j + 1, :] * pltpu.roll(vext, CONV_K - 1 - j, axis=0)[HALO:, :]
    br_b = _dot((bg * y).astype(BF16), wco_ref[...])

    ga = z_ref[:, 2 * d:3 * d].astype(F32)
    gb = z_ref[:, 3 * d:4 * d].astype(F32)
    merged = jax.nn.sigmoid(ga) * br_a + jax.nn.sigmoid(gb) * br_b
    o_ref[...] = x_ref[...] + _dot(merged.astype(BF16), wmo_ref[...])


def _mixer_mid(x, z, pool_mix, pool_scale, w_pool_out, conv_w, w_conv_out, w_mix_out, seq, tm):
    n, d = x.shape
    c = z.shape[1]
    hb = tm // HALO
    kern = functools.partial(_mixer_mid_kernel, tiles_per_seq=seq // tm, tm=tm, d=d)
    return pl.pallas_call(
        kern,
        grid=(n // tm,),
        in_specs=[
            pl.BlockSpec((tm, d), lambda i: (i, 0)),
            pl.BlockSpec((tm, c), lambda i: (i, 0)),
            pl.BlockSpec((HALO, c // 2), lambda i: (jnp.maximum(i * hb - 1, 0), 0)),
            _resident(pool_mix.shape),
            _resident(pool_scale.shape),
            _resident(w_pool_out.shape),
            _resident(conv_w.shape),
            _resident(w_conv_out.shape),
            _resident(w_mix_out.shape),
        ],
        out_specs=pl.BlockSpec((tm, d), lambda i: (i, 0)),
        out_shape=jax.ShapeDtypeStruct((n, d), F32),
        compiler_params=_params("arbitrary"),
        name="mixer_mid",
    )(x, z, z, pool_mix, pool_scale, w_pool_out, conv_w, w_conv_out, w_mix_out)


def _kv_kernel(m_ref, g_ref, w_ref, o_ref):
    mn = _rms(m_ref[...], g_ref[...]).astype(BF16)
    o_ref[...] = _dot(mn, w_ref[...]).astype(o_ref.dtype)


def _kv_proj(mem, g, w_kv, tn):
    r, d = mem.shape
    c = w_kv.shape[1]
    return pl.pallas_call(
        _kv_kernel,
        grid=(c // tn,),
        in_specs=[
            pl.BlockSpec((r, d), lambda j: (0, 0)),
            pl.BlockSpec((1, d), lambda j: (0, 0)),
            pl.BlockSpec((d, tn), lambda j: (0, j)),
        ],
        out_specs=pl.BlockSpec((r, tn), lambda j: (0, j)),
        out_shape=jax.ShapeDtypeStruct((r, c), BF16),
        compiler_params=_params("arbitrary"),
        name="kv_proj",
    )(mem, g, w_kv)


def _xattn_kernel(x_ref, g_ref, wq_ref, kv_ref, wo_ref, o_ref, *, d):
    hd = d // XA_HEADS
    x = x_ref[...]
    q = _dot(_rms(x, g_ref[...]).astype(BF16), wq_ref[...]).astype(BF16)
    scale = 1.0 / math.sqrt(hd)
    heads = []
    for h in range(XA_HEADS):
        qh = q[:, h * hd:(h + 1) * hd]
        kh = kv_ref[:, h * hd:(h + 1) * hd]
        vh = kv_ref[:, d + h * hd:d + (h + 1) * hd]
        s = lax.dot_general(qh, kh, (((1,), (1,)), ((), ())), preferred_element_type=F32) * scale
        e = jnp.exp(s - jnp.max(s, axis=-1, keepdims=True))
        p = (e / jnp.sum(e, axis=-1, keepdims=True)).astype(BF16)
        heads.append(_dot(p, vh).astype(BF16))
    o_ref[...] = x + _dot(jnp.concatenate(heads, axis=1), wo_ref[...])


def _xattn(x, g, w_q, kv, w_o, seq, n_mem, tm):
    n, d = x.shape
    tiles_per_seq = seq // tm
    return pl.pallas_call(
        functools.partial(_xattn_kernel, d=d),
        grid=(n // tm,),
        in_specs=[
            pl.BlockSpec((tm, d), lambda i: (i, 0)),
            pl.BlockSpec((1, d), lambda i: (0, 0)),
            _resident(w_q.shape),
            pl.BlockSpec((n_mem, 2 * d), lambda i: (i // tiles_per_seq, 0)),
            _resident(w_o.shape),
        ],
        out_specs=pl.BlockSpec((tm, d), lambda i: (i, 0)),
        out_shape=jax.ShapeDtypeStruct((n, d), F32),
        compiler_params=_params("arbitrary"),
        name="xattn",
    )(x, g, w_q, kv, w_o)


def _route_kernel(x_ref, g_ref, wr_ref, br_ref, h_ref, r_ref):
    h = _rms(x_ref[...], g_ref[...])
    h_ref[...] = h.astype(h_ref.dtype)
    logits = jnp.dot(h, wr_ref[...], preferred_element_type=F32, precision=lax.Precision.HIGHEST) + br_ref[...]
    lane = lax.broadcasted_iota(jnp.int32, logits.shape, 1)
    neg = jnp.float32(-jnp.inf)
    big = jnp.int32(ROUTE_LANES)

    is_g = (lane >= N_EXPERTS) & (lane < N_EXPERTS + N_GROUPS)
    lg = jnp.where(is_g, logits, neg)
    gmax = jnp.max(lg, axis=-1, keepdims=True)
    g_sel = jnp.min(jnp.where(lg == gmax, lane - N_EXPERTS, big), axis=-1, keepdims=True)
    p_group = 1.0 / jnp.sum(jnp.exp(lg - gmax), axis=-1, keepdims=True)

    in_group = (lane >= g_sel * EXPERTS_PER_GROUP) & (lane < (g_sel + 1) * EXPERTS_PER_GROUP)
    le = jnp.where(in_group, logits, neg)
    v1 = jnp.max(le, axis=-1, keepdims=True)
    i1 = jnp.min(jnp.where(le == v1, lane, big), axis=-1, keepdims=True)
    le2 = jnp.where(lane == i1, neg, le)
    v2 = jnp.max(le2, axis=-1, keepdims=True)
    i2 = jnp.min(jnp.where(le2 == v2, lane, big), axis=-1, keepdims=True)
    t = jnp.exp(v2 - v1)
    g1 = p_group / (1.0 + t)
    g2 = p_group * t / (1.0 + t)

    out = jnp.where(lane == 0, i1.astype(F32), 0.0)
    out = jnp.where(lane == 1, i2.astype(F32), out)
    out = jnp.where(lane == 2, g1, out)
    out = jnp.where(lane == 3, g2, out)
    r_ref[...] = out


def _route(x, g, w_r, b_r, tm):
    n, d = x.shape
    return pl.pallas_call(
        _route_kernel,
        grid=(n // tm,),
        in_specs=[
            pl.BlockSpec((tm, d), lambda i: (i, 0)),
            pl.BlockSpec((1, d), lambda i: (0, 0)),
            pl.BlockSpec((d, ROUTE_LANES), lambda i: (0, 0)),
            pl.BlockSpec((1, ROUTE_LANES), lambda i: (0, 0)),
        ],
        out_specs=[
            pl.BlockSpec((tm, d), lambda i: (i, 0)),
            pl.BlockSpec((tm, ROUTE_LANES), lambda i: (i, 0)),
        ],
        out_shape=[
            jax.ShapeDtypeStruct((n, d), BF16),
            jax.ShapeDtypeStruct((n, ROUTE_LANES), F32),
        ],
        compiler_params=_params("arbitrary"),
        name="route",
    )(x, g, w_r, b_r)


def _gmm_kernel(te_ref, nu_ref, x_ref, gate_ref, wg_ref, wu_ref, wd_ref, y_ref):
    used = pl.program_id(0) < nu_ref[0]

    @pl.when(used)
    def _():
        x = x_ref[...]
        a = _dot(x, wg_ref[0])
        b = _dot(x, wu_ref[0])
        hid = (a * jax.nn.sigmoid(a) * b).astype(BF16)
        y_ref[...] = (_dot(hid, wd_ref[0]) * gate_ref[...]).astype(y_ref.dtype)

    @pl.when(jnp.logical_not(used))
    def _():
        y_ref[...] = jnp.zeros_like(y_ref)


def _gmm(tile_expert, n_used, x_pad, gate_pad, w_gate, w_up, w_down, tm):
    rows, d = x_pad.shape
    de = w_gate.shape[2]
    grid_spec = pltpu.PrefetchScalarGridSpec(
        num_scalar_prefetch=2,
        grid=(rows // tm,),
        in_specs=[
            pl.BlockSpec((tm, d), lambda i, te, nu: (i, 0)),
            pl.BlockSpec((tm, 1), lambda i, te, nu: (i, 0)),
            pl.BlockSpec((1, d, de), lambda i, te, nu: (te[i], 0, 0)),
            pl.BlockSpec((1, d, de), lambda i, te, nu: (te[i], 0, 0)),
            pl.BlockSpec((1, de, d), lambda i, te, nu: (te[i], 0, 0)),
        ],
        out_specs=pl.BlockSpec((tm, d), lambda i, te, nu: (i, 0)),
    )
    return pl.pallas_call(
        _gmm_kernel,
        grid_spec=grid_spec,
        out_shape=jax.ShapeDtypeStruct((rows, d), BF16),
        compiler_params=_params("arbitrary"),
        name="gmm",
    )(tile_expert, n_used, x_pad, gate_pad, w_gate, w_up, w_down)


def _combine_kernel(x_ref, y_ref, g_ref, o_ref, *, d, final_norm):
    x = x_ref[...] + y_ref[:, :d].astype(F32) + y_ref[:, d:].astype(F32)
    o_ref[...] = _rms(x, g_ref[...]) if final_norm else x


def _combine(x, y2, g, tm, final_norm):
    n, d = x.shape
    return pl.pallas_call(
        functools.partial(_combine_kernel, d=d, final_norm=final_norm),
        grid=(n // tm,),
        in_specs=[
            pl.BlockSpec((tm, d), lambda i: (i, 0)),
            pl.BlockSpec((tm, 2 * d), lambda i: (i, 0)),
            pl.BlockSpec((1, d), lambda i: (0, 0)),
        ],
        out_specs=pl.BlockSpec((tm, d), lambda i: (i, 0)),
        out_shape=jax.ShapeDtypeStruct((n, d), F32),
        compiler_params=_params("arbitrary"),
        name="combine",
    )(x, y2, g)


def _moe_plan(e_flat, g_flat, tm):
    a = e_flat.shape[0]
    onehot = (e_flat[:, None] == jnp.arange(N_EXPERTS, dtype=jnp.int32)[None, :]).astype(jnp.int32)
    csum = jnp.cumsum(onehot, axis=0)
    counts = csum[-1]
    rank = jnp.sum(csum * onehot, axis=1) - 1
    padded = (counts + tm - 1) // tm * tm
    pends = jnp.cumsum(padded)
    pstarts = pends - padded
    dest = pstarts[e_flat] + rank
    rows = a + N_EXPERTS * tm
    n_tiles = rows // tm
    n_used = (pends[-1] // tm).astype(jnp.int32)
    tile_start = jnp.arange(n_tiles, dtype=jnp.int32) * tm
    te = jnp.clip(jnp.searchsorted(pends, tile_start, side="right"), 0, N_EXPERTS - 1).astype(jnp.int32)
    te = jnp.where(jnp.arange(n_tiles) < n_used, te, te[jnp.maximum(n_used - 1, 0)])
    src = jnp.zeros((rows,), jnp.int32).at[dest].set(jnp.arange(a, dtype=jnp.int32) // 2)
    gate_pad = jnp.zeros((rows,), F32).at[dest].set(g_flat)
    return dest, src, gate_pad, te, n_used.reshape(1)


def _tile(n, want):
    return min(n, want)


def kernel(x, mem, w_in, pool_mix, pool_scale, w_pool_out, conv_w, w_conv_out, w_mix_out, g_mix, g_xattn, g_mem, w_q, w_k, w_v, w_o, g_ffn, w_route_group, b_route_group, w_route_expert, b_route_expert, w_gate, w_up, w_down, g_final):
    bsz, seq, d = x.shape
    n_mem = mem.shape[1]
    depth = w_in.shape[0]
    n = bsz * seq
    xf = x.reshape(n, d)
    memf = mem.reshape(bsz * n_mem, d)

    tm_in = _tile(seq, 1024)
    tm_mid = _tile(seq, 256)
    tm_att = _tile(seq, 256)
    tm_route = _tile(seq, 512)
    tm_gmm = 256
    tm_comb = _tile(seq, 512)

    pad_lanes = ROUTE_LANES - N_EXPERTS - N_GROUPS
    for l in range(depth):
        row = lambda v: v[l].reshape(1, -1)
        z = _mixer_in(xf, row(g_mix), w_in[l].astype(BF16), tm_in, _tile(w_in.shape[2], 1024))
        xf = _mixer_mid(xf, z, pool_mix[l].astype(BF16), row(pool_scale), w_pool_out[l].astype(BF16),
                        conv_w[l], w_conv_out[l].astype(BF16), w_mix_out[l].astype(BF16), seq, tm_mid)

        w_kv = jnp.concatenate([w_k[l], w_v[l]], axis=1).astype(BF16)
        kv = _kv_proj(memf, row(g_mem), w_kv, _tile(2 * d, 1024))
        xf = _xattn(xf, row(g_xattn), w_q[l].astype(BF16), kv, w_o[l].astype(BF16), seq, n_mem, tm_att)

        w_r = jnp.concatenate([w_route_expert[l], w_route_group[l], jnp.zeros((d, pad_lanes), F32)], axis=1)
        b_r = jnp.concatenate([b_route_expert[l], b_route_group[l], jnp.zeros((pad_lanes,), F32)]).reshape(1, -1)
        h, route = _route(xf, row(g_ffn), w_r, b_r, tm_route)

        e_flat = route[:, 0:2].astype(jnp.int32).reshape(2 * n)
        g_flat = route[:, 2:4].reshape(2 * n)
        dest, src, gate_pad, te, n_used = _moe_plan(e_flat, g_flat, tm_gmm)
        x_pad = jnp.take(h, src, axis=0)
        y_pad = _gmm(te, n_used, x_pad, gate_pad.reshape(-1, 1), w_gate[l].astype(BF16), w_up[l].astype(BF16),
                     w_down[l].astype(BF16), tm_gmm)
        y2 = jnp.take(y_pad, dest, axis=0).reshape(n, 2 * d)
        xf = _combine(xf, y2, g_final.reshape(1, -1), tm_comb, final_norm=(l == depth - 1))
    return xf.reshape(bsz, seq, d)
```

```python
import functools
import math

import jax
import jax.numpy as jnp
from jax import lax
from jax.experimental import pallas as pl
from jax.experimental.pallas import tpu as pltpu

EPS = 1e-6
POOL_WINDOWS = (2, 4, 8, 16)
CONV_K = 3
XA_HEADS = 4
N_GROUPS = 4
EXPERTS_PER_GROUP = 8
N_EXPERTS = N_GROUPS * EXPERTS_PER_GROUP
HALO = 16
ROUTE_LANES = 128
VMEM_LIMIT = 56 * 1024 * 1024

F32 = jnp.float32
BF16 = jnp.bfloat16


def _params(*sem):
    return pltpu.CompilerParams(dimension_semantics=sem, vmem_limit_bytes=VMEM_LIMIT)


def _resident(shape):
    nd = len(shape)
    return pl.BlockSpec(shape, lambda *_: (0,) * nd, pipeline_mode=pl.Buffered(1))


def _rms(xf, g):
    ms = jnp.mean(xf * xf, axis=-1, keepdims=True)
    return xf * lax.rsqrt(ms + EPS) * g


def _dot(a, b):
    return jnp.dot(a, b, preferred_element_type=F32)


def _mixer_in_kernel(x_ref, g_ref, w_ref, z_ref, h_scr):
    @pl.when(pl.program_id(1) == 0)
    def _():
        h_scr[...] = _rms(x_ref[...], g_ref[...]).astype(BF16)

    z_ref[...] = _dot(h_scr[...], w_ref[...]).astype(z_ref.dtype)


def _mixer_in(x, g, w, tm, tn):
    n, d = x.shape
    c = w.shape[1]
    return pl.pallas_call(
        _mixer_in_kernel,
        grid=(n // tm, c // tn),
        in_specs=[
            pl.BlockSpec((tm, d), lambda i, j: (i, 0)),
            pl.BlockSpec((1, d), lambda i, j: (0, 0)),
            pl.BlockSpec((d, tn), lambda i, j: (0, j)),
        ],
        out_specs=pl.BlockSpec((tm, tn), lambda i, j: (i, j)),
        out_shape=jax.ShapeDtypeStruct((n, c), BF16),
        scratch_shapes=[pltpu.VMEM((tm, d), BF16)],
        compiler_params=_params("arbitrary", "arbitrary"),
        name="mixer_in",
    )(x, g, w)


def _mixer_mid_kernel(x_ref, z_ref, zh_ref, pm_ref, ps_ref, wpo_ref, cw_ref, wco_ref, wmo_ref, o_ref,
                      *, tiles_per_seq, tm, d):
    pw = d // 2
    gw = pw // len(POOL_WINDOWS)
    t = pl.program_id(0) % tiles_per_seq
    keep = jnp.where(t == 0, 0.0, 1.0).astype(F32)
    pos = lax.broadcasted_iota(jnp.int32, (tm, gw), 0) + (t * tm + 1)

    parts = []
    for gi, w in enumerate(POOL_WINDOWS):
        cols = slice(gi * gw, (gi + 1) * gw)
        u = z_ref[:, cols].astype(F32)
        ext = jnp.concatenate([zh_ref[:, cols].astype(F32) * keep, u], axis=0)
        s = 1
        while s < w:
            ext = ext + pltpu.roll(ext, s, axis=0)
            s *= 2
        cnt = jnp.minimum(pos, w).astype(F32)
        p = (ext[HALO:, :] / cnt - u).astype(BF16)
        m = _dot(p, pm_ref[gi]) * ps_ref[:, cols]
        parts.append(m.astype(BF16))
    br_a = _dot(jnp.concatenate(parts, axis=1), wpo_ref[...])

    uc = z_ref[:, pw:2 * pw].astype(F32)
    bg = z_ref[:, 2 * pw:3 * pw].astype(F32)
    cg = z_ref[:, 3 * pw:4 * pw].astype(F32)
    v = cg * uc
    vh = zh_ref[:, 3 * pw:4 * pw].astype(F32) * zh_ref[:, pw:2 * pw].astype(F32) * keep
    vext = jnp.concatenate([vh, v], axis=0)
    y = cw_ref[CONV_K - 1:CONV_K, :] * v
    for j in range(CONV_K - 1):
        y = y + cw_ref[j:j + 1, :] * pltpu.roll(vext, CONV_K - 1 - j, axis=0)[HALO:, :]
    br_b = _dot((bg * y).astype(BF16), wco_ref[...])

    ga = z_ref[:, 2 * d:3 * d].astype(F32)
    gb = z_ref[:, 3 * d:4 * d].astype(F32)
    merged = jax.nn.sigmoid(ga) * br_a + jax.nn.sigmoid(gb) * br_b
    o_ref[...] = x_ref[...] + _dot(merged.astype(BF16), wmo_ref[...])


def _mixer_mid(x, z, pool_mix, pool_scale, w_pool_out, conv_w, w_conv_out, w_mix_out, seq, tm):
    n, d = x.shape
    c = z.shape[1]
    hb = tm // HALO
    kern = functools.partial(_mixer_mid_kernel, tiles_per_seq=seq // tm, tm=tm, d=d)
    return pl.pallas_call(
        kern,
        grid=(n // tm,),
        in_specs=[
            pl.BlockSpec((tm, d), lambda i: (i, 0)),
            pl.BlockSpec((tm, c), lambda i: (i, 0)),
            pl.BlockSpec((HALO, c // 2), lambda i: (jnp.maximum(i * hb - 1, 0), 0)),
            _resident(pool_mix.shape),
            _resident(pool_scale.shape),
            _resident(w_pool_out.shape),
            _resident(conv_w.shape),
            _resident(w_conv_out.shape),
            _resident(w_mix_out.shape),
        ],
        out_specs=pl.BlockSpec((tm, d), lambda i: (i, 0)),
        out_shape=jax.ShapeDtypeStruct((n, d), F32),
        compiler_params=_params("arbitrary"),
        name="mixer_mid",
    )(x, z, z, pool_mix, pool_scale, w_pool_out, conv_w, w_conv_out, w_mix_out)


def _kv_kernel(m_ref, g_ref, w_ref, o_ref):
    mn = _rms(m_ref[...], g_ref[...]).astype(BF16)
    o_ref[...] = _dot(mn, w_ref[...]).astype(o_ref.dtype)


def _kv_proj(mem, g, w_kv, tn):
    r, d = mem.shape
    c = w_kv.shape[1]
    return pl.pallas_call(
        _kv_kernel,
        grid=(c // tn,),
        in_specs=[
            pl.BlockSpec((r, d), lambda j: (0, 0)),
            pl.BlockSpec((1, d), lambda j: (0, 0)),
            pl.BlockSpec((d, tn), lambda j: (0, j)),
        ],
        out_specs=pl.BlockSpec((r, tn), lambda j: (0, j)),
        out_shape=jax.ShapeDtypeStruct((r, c), BF16),
        compiler_params=_params("arbitrary"),
        name="kv_proj",
    )(mem, g, w_kv)


def _xattn_kernel(x_ref, g_ref, wq_ref, kv_ref, wo_ref, o_ref, *, d):
    hd = d // XA_HEADS
    x = x_ref[...]
    q = _dot(_rms(x, g_ref[...]).astype(BF16), wq_ref[...]).astype(BF16)
    scale = 1.0 / math.sqrt(hd)
    heads = []
    for h in range(XA_HEADS):
        qh = q[:, h * hd:(h + 1) * hd]
        kh = kv_ref[:, h * hd:(h + 1) * hd]
        vh = kv_ref[:, d + h * hd:d + (h + 1) * hd]
        s = lax.dot_general(qh, kh, (((1,), (1,)), ((), ())), preferred_element_type=F32) * scale
        e = jnp.exp(s - jnp.max(s, axis=-1, keepdims=True))
        p = (e / jnp.sum(e, axis=-1, keepdims=True)).astype(BF16)
        heads.append(_dot(p, vh).astype(BF16))
    o_ref[...] = x + _dot(jnp.concatenate(heads, axis=1), wo_ref[...])


def _xattn(x, g, w_q, kv, w_o, seq, n_mem, tm):
    n, d = x.shape
    tiles_per_seq = seq // tm
    return pl.pallas_call(
        functools.partial(_xattn_kernel, d=d),
        grid=(n // tm,),
        in_specs=[
            pl.BlockSpec((tm, d), lambda i: (i, 0)),
            pl.BlockSpec((1, d), lambda i: (0, 0)),
            _resident(w_q.shape),
            pl.BlockSpec((n_mem, 2 * d), lambda i: (i // tiles_per_seq, 0)),
            _resident(w_o.shape),
        ],
        out_specs=pl.BlockSpec((tm, d), lambda i: (i, 0)),
        out_shape=jax.ShapeDtypeStruct((n, d), F32),
        compiler_params=_params("arbitrary"),
        name="xattn",
    )(x, g, w_q, kv, w_o)


def _pack_pairs(v):
    c = v.shape[1] // 2
    bits = lax.bitcast_convert_type(v.astype(BF16).astype(F32), jnp.uint32)
    return (bits[:, :c] >> 16) | bits[:, c:]


def _unpack_pairs(w):
    lo = lax.bitcast_convert_type(w << 16, F32)
    hi = lax.bitcast_convert_type(w & jnp.uint32(0xFFFF0000), F32)
    return lo, hi


ROW_E1, ROW_E2, ROW_R1_HI, ROW_R1_LO, ROW_R2_HI, ROW_R2_LO = range(6)
ROUTE_ROWS = 8
RANK_RADIX = 256


def _route_kernel(x_ref, g_ref, wr_ref, br_ref, hp_ref, r_ref, rows_ref, cnt_ref, carry_scr, *, tm):
    @pl.when(pl.program_id(0) == 0)
    def _():
        carry_scr[...] = jnp.zeros_like(carry_scr)

    h = _rms(x_ref[...], g_ref[...])
    hp_ref[...] = _pack_pairs(h)
    logits = jnp.dot(h, wr_ref[...], preferred_element_type=F32, precision=lax.Precision.HIGHEST) + br_ref[...]
    lane = lax.broadcasted_iota(jnp.int32, logits.shape, 1)
    neg = jnp.float32(-jnp.inf)
    big = jnp.int32(ROUTE_LANES)

    is_g = (lane >= N_EXPERTS) & (lane < N_EXPERTS + N_GROUPS)
    lg = jnp.where(is_g, logits, neg)
    gmax = jnp.max(lg, axis=-1, keepdims=True)
    g_sel = jnp.min(jnp.where(lg == gmax, lane - N_EXPERTS, big), axis=-1, keepdims=True)
    p_group = 1.0 / jnp.sum(jnp.exp(lg - gmax), axis=-1, keepdims=True)

    in_group = (lane >= g_sel * EXPERTS_PER_GROUP) & (lane < (g_sel + 1) * EXPERTS_PER_GROUP)
    le = jnp.where(in_group, logits, neg)
    v1 = jnp.max(le, axis=-1, keepdims=True)
    i1 = jnp.min(jnp.where(le == v1, lane, big), axis=-1, keepdims=True)
    le2 = jnp.where(lane == i1, neg, le)
    v2 = jnp.max(le2, axis=-1, keepdims=True)
    i2 = jnp.min(jnp.where(le2 == v2, lane, big), axis=-1, keepdims=True)
    t = jnp.exp(v2 - v1)
    g1 = p_group / (1.0 + t)
    g2 = p_group * t / (1.0 + t)

    oh1 = lane == i1
    oh2 = lane == i2
    both = jnp.where(oh1, 1.0, 0.0) + jnp.where(oh2, 1.0, 0.0)
    rr = lax.broadcasted_iota(jnp.int32, (tm, tm), 0)
    cc = lax.broadcasted_iota(jnp.int32, (tm, tm), 1)
    lower = jnp.where(cc < rr, 1.0, 0.0).astype(BF16)
    before = _dot(lower, both.astype(BF16)) + carry_scr[...]
    rank1 = jnp.sum(jnp.where(oh1, before, 0.0), axis=-1, keepdims=True)
    rank2 = jnp.sum(jnp.where(oh2, before, 0.0), axis=-1, keepdims=True)
    carry_scr[...] = carry_scr[...] + jnp.sum(both, axis=0, keepdims=True)
    cnt_ref[...] = carry_scr[...]

    out = jnp.where(lane == 2, g1, 0.0)
    out = jnp.where(lane == 3, g2, out)
    r_ref[...] = out

    r1_hi = jnp.floor(rank1 * (1.0 / RANK_RADIX))
    r2_hi = jnp.floor(rank2 * (1.0 / RANK_RADIX))
    vals = jnp.where(lane == ROW_E1, i1.astype(F32), 0.0)
    vals = jnp.where(lane == ROW_E2, i2.astype(F32), vals)
    vals = jnp.where(lane == ROW_R1_HI, r1_hi, vals)
    vals = jnp.where(lane == ROW_R1_LO, rank1 - r1_hi * RANK_RADIX, vals)
    vals = jnp.where(lane == ROW_R2_HI, r2_hi, vals)
    vals = jnp.where(lane == ROW_R2_LO, rank2 - r2_hi * RANK_RADIX, vals)
    pick = jnp.where(lax.broadcasted_iota(jnp.int32, (ROUTE_ROWS, ROUTE_LANES), 0)
                     == lax.broadcasted_iota(jnp.int32, (ROUTE_ROWS, ROUTE_LANES), 1), 1.0, 0.0).astype(BF16)
    rows_ref[...] = lax.dot_general(pick, vals.astype(BF16), (((1,), (1,)), ((), ())),
                                    preferred_element_type=F32)


def _route(x, g, w_r, b_r, tm):
    n, d = x.shape
    return pl.pallas_call(
        functools.partial(_route_kernel, tm=tm),
        grid=(n // tm,),
        in_specs=[
            pl.BlockSpec((tm, d), lambda i: (i, 0)),
            pl.BlockSpec((1, d), lambda i: (0, 0)),
            pl.BlockSpec((d, ROUTE_LANES), lambda i: (0, 0)),
            pl.BlockSpec((1, ROUTE_LANES), lambda i: (0, 0)),
        ],
        out_specs=[
            pl.BlockSpec((tm, d // 2), lambda i: (i, 0)),
            pl.BlockSpec((tm, ROUTE_LANES), lambda i: (i, 0)),
            pl.BlockSpec((ROUTE_ROWS, tm), lambda i: (0, i)),
            pl.BlockSpec((1, ROUTE_LANES), lambda i: (0, 0)),
        ],
        out_shape=[
            jax.ShapeDtypeStruct((n, d // 2), jnp.uint32),
            jax.ShapeDtypeStruct((n, ROUTE_LANES), F32),
            jax.ShapeDtypeStruct((ROUTE_ROWS, n), F32),
            jax.ShapeDtypeStruct((1, ROUTE_LANES), F32),
        ],
        scratch_shapes=[pltpu.VMEM((1, ROUTE_LANES), F32)],
        compiler_params=_params("arbitrary"),
        name="route",
    )(x, g, w_r, b_r)


def _moe_plan(rows, cnt, tm):
    n = rows.shape[1]
    ri = rows.astype(jnp.int32)
    experts = jnp.arange(N_EXPERTS, dtype=jnp.int32)
    counts = cnt[0, :N_EXPERTS].astype(jnp.int32)
    padded = (counts + tm - 1) // tm * tm
    pends = jnp.cumsum(padded)
    pstarts = pends - padded

    def dest(e, hi, lo):
        start = jnp.sum(jnp.where(e[:, None] == experts[None, :], pstarts[None, :], 0), axis=1)
        return start + hi * RANK_RADIX + lo

    dest_flat = jnp.concatenate([dest(ri[ROW_E1], ri[ROW_R1_HI], ri[ROW_R1_LO]),
                                 dest(ri[ROW_E2], ri[ROW_R2_HI], ri[ROW_R2_LO])])
    n_tiles = (2 * n + N_EXPERTS * tm) // tm
    n_used = (pends[-1] // tm).astype(jnp.int32)
    tile_start = jnp.arange(n_tiles, dtype=jnp.int32) * tm
    te = jnp.clip(jnp.searchsorted(pends, tile_start, side="right"), 0, N_EXPERTS - 1).astype(jnp.int32)
    te = jnp.where(jnp.arange(n_tiles) < n_used, te, te[jnp.maximum(n_used - 1, 0)])
    return dest_flat, counts, pends, te, n_used.reshape(1)


DMA_UNROLL = 8


def _dispatch_kernel(dest_ref, cnt_ref, pend_ref, h_ref, xp_hbm, zero_scr, sem, zsem, *, tm, tile, n):
    i = pl.program_id(0)

    @pl.when(i == 0)
    def _():
        zero_scr[...] = jnp.zeros_like(zero_scr)

        def zero_tile(t, carry):
            cp = pltpu.make_async_copy(zero_scr, xp_hbm.at[pl.ds(pl.multiple_of(t * tile, tile), tile)], zsem)
            cp.start()
            cp.wait()
            return carry

        lax.fori_loop(pend_ref[N_EXPERTS - 1] // tile, xp_hbm.shape[0] // tile, zero_tile, 0)
        for e in range(N_EXPERTS):
            @pl.when(cnt_ref[e] > 0)
            def _():
                start = pl.multiple_of(pend_ref[e] - tile, tile)
                pltpu.make_async_copy(zero_scr, xp_hbm.at[pl.ds(start, tile)], zsem).start()
        for e in range(N_EXPERTS):
            @pl.when(cnt_ref[e] > 0)
            def _():
                start = pl.multiple_of(pend_ref[e] - tile, tile)
                pltpu.make_async_copy(zero_scr, xp_hbm.at[pl.ds(start, tile)], zsem).wait()

    base = i * tm

    def row_copy(r, k):
        return pltpu.make_async_copy(h_ref.at[pl.ds(r, 1)], xp_hbm.at[pl.ds(dest_ref[k * n + base + r], 1)], sem)

    def start_body(c, carry):
        for u in range(DMA_UNROLL):
            r = c * DMA_UNROLL + u
            row_copy(r, 0).start()
            row_copy(r, 1).start()
        return carry

    def wait_body(c, carry):
        for u in range(DMA_UNROLL):
            r = c * DMA_UNROLL + u
            row_copy(r, 0).wait()
            row_copy(r, 1).wait()
        return carry

    lax.fori_loop(0, tm // DMA_UNROLL, start_body, 0)
    lax.fori_loop(0, tm // DMA_UNROLL, wait_body, 0)


def _dispatch(dest_flat, counts, pends, hp, tm, tile):
    n, c = hp.shape
    rows = 2 * n + N_EXPERTS * tile
    grid_spec = pltpu.PrefetchScalarGridSpec(
        num_scalar_prefetch=3,
        grid=(n // tm,),
        in_specs=[pl.BlockSpec((tm, c), lambda i, *_: (i, 0))],
        out_specs=pl.BlockSpec(memory_space=pl.ANY),
        scratch_shapes=[pltpu.VMEM((tile, c), jnp.uint32), pltpu.SemaphoreType.DMA, pltpu.SemaphoreType.DMA],
    )
    return pl.pallas_call(
        functools.partial(_dispatch_kernel, tm=tm, tile=tile, n=n),
        grid_spec=grid_spec,
        out_shape=jax.ShapeDtypeStruct((rows, c), jnp.uint32),
        compiler_params=_params("arbitrary"),
        name="dispatch",
    )(dest_flat, counts, pends, hp)


def _gmm_kernel(te_ref, nu_ref, x_ref, wg_ref, wu_ref, wd_ref, y_ref):
    used = pl.program_id(0) < nu_ref[0]

    @pl.when(used)
    def _():
        lo, hi = _unpack_pairs(x_ref[...])
        x = jnp.concatenate([lo.astype(BF16), hi.astype(BF16)], axis=1)
        a = _dot(x, wg_ref[0])
        b = _dot(x, wu_ref[0])
        hid = (a * jax.nn.sigmoid(a) * b).astype(BF16)
        y_ref[...] = _pack_pairs(_dot(hid, wd_ref[0]))

    @pl.when(jnp.logical_not(used))
    def _():
        y_ref[...] = jnp.zeros_like(y_ref)


def _gmm(tile_expert, n_used, x_pad, w_gate, w_up, w_down, tm):
    rows, c = x_pad.shape
    _, d, de = w_gate.shape

    def row_map(i, te, nu):
        return (jnp.minimum(i, jnp.maximum(nu[0] - 1, 0)), 0)

    grid_spec = pltpu.PrefetchScalarGridSpec(
        num_scalar_prefetch=2,
        grid=(rows // tm,),
        in_specs=[
            pl.BlockSpec((tm, c), row_map),
            pl.BlockSpec((1, d, de), lambda i, te, nu: (te[i], 0, 0)),
            pl.BlockSpec((1, d, de), lambda i, te, nu: (te[i], 0, 0)),
            pl.BlockSpec((1, de, d), lambda i, te, nu: (te[i], 0, 0)),
        ],
        out_specs=pl.BlockSpec((tm, c), lambda i, te, nu: (i, 0)),
    )
    return pl.pallas_call(
        _gmm_kernel,
        grid_spec=grid_spec,
        out_shape=jax.ShapeDtypeStruct((rows, c), jnp.uint32),
        compiler_params=_params("arbitrary"),
        name="gmm",
    )(tile_expert, n_used, x_pad, w_gate, w_up, w_down)


def _combine_kernel(dest_ref, x_ref, r_ref, g_ref, yp_hbm, o_ref, ybuf, sem, *, tm, n, final_norm):
    i = pl.program_id(0)
    n_steps = pl.num_programs(0)
    c = ybuf.shape[-1]

    def row_copy(step, slot, r, k):
        src = yp_hbm.at[pl.ds(dest_ref[k * n + step * tm + r], 1)]
        return pltpu.make_async_copy(src, ybuf.at[slot, k, pl.ds(r, 1)], sem.at[slot])

    def gather(step, slot, wait):
        def body(cidx, carry):
            for u in range(DMA_UNROLL):
                r = cidx * DMA_UNROLL + u
                for k in range(2):
                    cp = row_copy(step, slot, r, k)
                    cp.wait() if wait else cp.start()
            return carry
        lax.fori_loop(0, tm // DMA_UNROLL, body, 0)

    @pl.when(i == 0)
    def _():
        gather(0, 0, wait=False)

    @pl.when(i + 1 < n_steps)
    def _():
        gather(i + 1, (i + 1) % 2, wait=False)

    slot = i % 2
    gather(i, slot, wait=True)

    g1 = r_ref[:, 2:3]
    g2 = r_ref[:, 3:4]
    lo1, hi1 = _unpack_pairs(ybuf[slot, 0])
    lo2, hi2 = _unpack_pairs(ybuf[slot, 1])
    x = x_ref[...]
    out = jnp.concatenate([x[:, :c] + g1 * lo1 + g2 * lo2, x[:, c:] + g1 * hi1 + g2 * hi2], axis=1)
    o_ref[...] = _rms(out, g_ref[...]) if final_norm else out


def _combine(dest_flat, x, route, g, y_pad, tm, final_norm):
    n, d = x.shape
    c = y_pad.shape[1]
    grid_spec = pltpu.PrefetchScalarGridSpec(
        num_scalar_prefetch=1,
        grid=(n // tm,),
        in_specs=[
            pl.BlockSpec((tm, d), lambda i, *_: (i, 0)),
            pl.BlockSpec((tm, ROUTE_LANES), lambda i, *_: (i, 0)),
            pl.BlockSpec((1, d), lambda i, *_: (0, 0)),
            pl.BlockSpec(memory_space=pl.ANY),
        ],
        out_specs=pl.BlockSpec((tm, d), lambda i, *_: (i, 0)),
        scratch_shapes=[pltpu.VMEM((2, 2, tm, c), jnp.uint32), pltpu.SemaphoreType.DMA((2,))],
    )
    return pl.pallas_call(
        functools.partial(_combine_kernel, tm=tm, n=n, final_norm=final_norm),
        grid_spec=grid_spec,
        out_shape=jax.ShapeDtypeStruct((n, d), F32),
        compiler_params=_params("arbitrary"),
        name="combine",
    )(dest_flat, x, route, g, y_pad)


def _tile(n, want):
    return min(n, want)


def kernel(x, mem, w_in, pool_mix, pool_scale, w_pool_out, conv_w, w_conv_out, w_mix_out, g_mix, g_xattn, g_mem, w_q, w_k, w_v, w_o, g_ffn, w_route_group, b_route_group, w_route_expert, b_route_expert, w_gate, w_up, w_down, g_final):
    bsz, seq, d = x.shape
    n_mem = mem.shape[1]
    depth = w_in.shape[0]
    n = bsz * seq
    xf = x.reshape(n, d)
    memf = mem.reshape(bsz * n_mem, d)

    tm_in = _tile(seq, 1024)
    tm_mid = _tile(seq, 256)
    tm_att = _tile(seq, 256)
    tm_route = _tile(seq, 512)
    tm_disp = _tile(seq, 512)
    tm_gmm = 256
    tm_comb = _tile(seq, 256)

    pad_lanes = ROUTE_LANES - N_EXPERTS - N_GROUPS
    for l in range(depth):
        row = lambda v: v[l].reshape(1, -1)
        z = _mixer_in(xf, row(g_mix), w_in[l].astype(BF16), tm_in, _tile(w_in.shape[2], 1024))
        xf = _mixer_mid(xf, z, pool_mix[l].astype(BF16), row(pool_scale), w_pool_out[l].astype(BF16),
                        conv_w[l], w_conv_out[l].astype(BF16), w_mix_out[l].astype(BF16), seq, tm_mid)

        w_kv = jnp.concatenate([w_k[l], w_v[l]], axis=1).astype(BF16)
        kv = _kv_proj(memf, row(g_mem), w_kv, _tile(2 * d, 1024))
        xf = _xattn(xf, row(g_xattn), w_q[l].astype(BF16), kv, w_o[l].astype(BF16), seq, n_mem, tm_att)

        w_r = jnp.concatenate([w_route_expert[l], w_route_group[l], jnp.zeros((d, pad_lanes), F32)], axis=1)
        b_r = jnp.concatenate([b_route_expert[l], b_route_group[l], jnp.zeros((pad_lanes,), F32)]).reshape(1, -1)
        hp, route, rows, cnt = _route(xf, row(g_ffn), w_r, b_r, tm_route)
        dest_flat, counts, pends, te, n_used = _moe_plan(rows, cnt, tm_gmm)
        x_pad = _dispatch(dest_flat, counts, pends, hp, tm_disp, tm_gmm)
        y_pad = _gmm(te, n_used, x_pad, w_gate[l].astype(BF16), w_up[l].astype(BF16), w_down[l].astype(BF16), tm_gmm)
        xf = _combine(dest_flat, xf, route, g_final.reshape(1, -1), y_pad, tm_comb, final_norm=(l == depth - 1))
    return xf.reshape(bsz, seq, d)
```

```python
import functools
import math

import jax
import jax.numpy as jnp
from jax import lax
from jax.experimental import pallas as pl
from jax.experimental.pallas import tpu as pltpu

EPS = 1e-6
POOL_WINDOWS = (2, 4, 8, 16)
CONV_K = 3
XA_HEADS = 4
N_GROUPS = 4
EXPERTS_PER_GROUP = 8
N_EXPERTS = N_GROUPS * EXPERTS_PER_GROUP
HALO = 16
ROUTE_LANES = 128
VMEM_LIMIT = 56 * 1024 * 1024

F32 = jnp.float32
BF16 = jnp.bfloat16


def _params(*sem):
    return pltpu.CompilerParams(dimension_semantics=sem, vmem_limit_bytes=VMEM_LIMIT)


def _resident(stack, l):
    nd = stack.ndim - 1
    return pl.BlockSpec((None,) + stack.shape[1:], lambda *_: (l,) + (0,) * nd, pipeline_mode=pl.Buffered(1))


def _rms(xf, g):
    ms = jnp.mean(xf * xf, axis=-1, keepdims=True)
    return xf * lax.rsqrt(ms + EPS) * g


def _dot(a, b):
    return jnp.dot(a, b, preferred_element_type=F32)


def _mixer_in_kernel(x_ref, g_ref, w_ref, z_ref, h_scr):
    @pl.when(pl.program_id(1) == 0)
    def _():
        h_scr[...] = _rms(x_ref[...], g_ref[...]).astype(BF16)

    z_ref[...] = _dot(h_scr[...], w_ref[...]).astype(z_ref.dtype)


def _mixer_in(x, g, w, l, tm, tn):
    n, d = x.shape
    c = w.shape[2]
    return pl.pallas_call(
        _mixer_in_kernel,
        grid=(n // tm, c // tn),
        in_specs=[
            pl.BlockSpec((tm, d), lambda i, j: (i, 0)),
            pl.BlockSpec((1, d), lambda i, j: (0, 0)),
            pl.BlockSpec((None, d, tn), lambda i, j: (l, 0, j)),
        ],
        out_specs=pl.BlockSpec((tm, tn), lambda i, j: (i, j)),
        out_shape=jax.ShapeDtypeStruct((n, c), BF16),
        scratch_shapes=[pltpu.VMEM((tm, d), BF16)],
        compiler_params=_params("arbitrary", "arbitrary"),
        name="mixer_in",
    )(x, g, w)


def _mixer_mid_kernel(x_ref, z_ref, zh_ref, pm_ref, ps_ref, wpo_ref, cw_ref, wco_ref, wmo_ref, o_ref,
                      *, tiles_per_seq, tm, d):
    pw = d // 2
    gw = pw // len(POOL_WINDOWS)
    t = pl.program_id(0) % tiles_per_seq
    keep = jnp.where(t == 0, 0.0, 1.0).astype(F32)
    pos = lax.broadcasted_iota(jnp.int32, (tm, gw), 0) + (t * tm + 1)

    parts = []
    for gi, w in enumerate(POOL_WINDOWS):
        cols = slice(gi * gw, (gi + 1) * gw)
        u = z_ref[:, cols].astype(F32)
        ext = jnp.concatenate([zh_ref[:, cols].astype(F32) * keep, u], axis=0)
        s = 1
        while s < w:
            ext = ext + pltpu.roll(ext, s, axis=0)
            s *= 2
        cnt = jnp.minimum(pos, w).astype(F32)
        p = (ext[HALO:, :] / cnt - u).astype(BF16)
        m = _dot(p, pm_ref[gi]) * ps_ref[:, cols]
        parts.append(m.astype(BF16))
    br_a = _dot(jnp.concatenate(parts, axis=1), wpo_ref[...])

    uc = z_ref[:, pw:2 * pw].astype(F32)
    bg = z_ref[:, 2 * pw:3 * pw].astype(F32)
    cg = z_ref[:, 3 * pw:4 * pw].astype(F32)
    v = cg * uc
    vh = zh_ref[:, 3 * pw:4 * pw].astype(F32) * zh_ref[:, pw:2 * pw].astype(F32) * keep
    vext = jnp.concatenate([vh, v], axis=0)
    y = cw_ref[CONV_K - 1:CONV_K, :] * v
    for j in range(CONV_K - 1):
        y = y + cw_ref[j:j + 1, :] * pltpu.roll(vext, CONV_K - 1 - j, axis=0)[HALO:, :]
    br_b = _dot((bg * y).astype(BF16), wco_ref[...])

    ga = z_ref[:, 2 * d:3 * d].astype(F32)
    gb = z_ref[:, 3 * d:4 * d].astype(F32)
    merged = jax.nn.sigmoid(ga) * br_a + jax.nn.sigmoid(gb) * br_b
    o_ref[...] = x_ref[...] + _dot(merged.astype(BF16), wmo_ref[...])


def _mixer_mid(x, z, pool_mix, pool_scale, w_pool_out, conv_w, w_conv_out, w_mix_out, l, seq, tm):
    n, d = x.shape
    c = z.shape[1]
    hb = tm // HALO
    kern = functools.partial(_mixer_mid_kernel, tiles_per_seq=seq // tm, tm=tm, d=d)
    return pl.pallas_call(
        kern,
        grid=(n // tm,),
        in_specs=[
            pl.BlockSpec((tm, d), lambda i: (i, 0)),
            pl.BlockSpec((tm, c), lambda i: (i, 0)),
            pl.BlockSpec((HALO, c // 2), lambda i: (jnp.maximum(i * hb - 1, 0), 0)),
            _resident(pool_mix, l),
            _resident(pool_scale, l),
            _resident(w_pool_out, l),
            _resident(conv_w, l),
            _resident(w_conv_out, l),
            _resident(w_mix_out, l),
        ],
        out_specs=pl.BlockSpec((tm, d), lambda i: (i, 0)),
        out_shape=jax.ShapeDtypeStruct((n, d), F32),
        compiler_params=_params("arbitrary"),
        name="mixer_mid",
    )(x, z, z, pool_mix, pool_scale, w_pool_out, conv_w, w_conv_out, w_mix_out)


def _kv_kernel(m_ref, g_ref, wk_ref, wv_ref, o_ref, *, nb):
    mn = _rms(m_ref[...], g_ref[...]).astype(BF16)
    j = pl.program_id(0)

    @pl.when(j < nb)
    def _():
        o_ref[...] = _dot(mn, wk_ref[...]).astype(o_ref.dtype)

    @pl.when(j >= nb)
    def _():
        o_ref[...] = _dot(mn, wv_ref[...]).astype(o_ref.dtype)


def _kv_proj(mem, g, w_k, w_v, l, tn):
    r, d = mem.shape
    nb = w_k.shape[2] // tn
    return pl.pallas_call(
        functools.partial(_kv_kernel, nb=nb),
        grid=(2 * nb,),
        in_specs=[
            pl.BlockSpec((r, d), lambda j: (0, 0)),
            pl.BlockSpec((1, d), lambda j: (0, 0)),
            pl.BlockSpec((None, d, tn), lambda j: (l, 0, jnp.minimum(j, nb - 1))),
            pl.BlockSpec((None, d, tn), lambda j: (l, 0, jnp.maximum(j - nb, 0))),
        ],
        out_specs=pl.BlockSpec((r, tn), lambda j: (0, j)),
        out_shape=jax.ShapeDtypeStruct((r, 2 * nb * tn), BF16),
        compiler_params=_params("arbitrary"),
        name="kv_proj",
    )(mem, g, w_k, w_v)


def _xattn_kernel(x_ref, g_ref, wq_ref, kv_ref, wo_ref, o_ref, *, d):
    hd = d // XA_HEADS
    x = x_ref[...]
    q = _dot(_rms(x, g_ref[...]).astype(BF16), wq_ref[...]).astype(BF16)
    scale = 1.0 / math.sqrt(hd)
    heads = []
    for h in range(XA_HEADS):
        qh = q[:, h * hd:(h + 1) * hd]
        kh = kv_ref[:, h * hd:(h + 1) * hd]
        vh = kv_ref[:, d + h * hd:d + (h + 1) * hd]
        s = lax.dot_general(qh, kh, (((1,), (1,)), ((), ())), preferred_element_type=F32) * scale
        e = jnp.exp(s - jnp.max(s, axis=-1, keepdims=True))
        p = (e / jnp.sum(e, axis=-1, keepdims=True)).astype(BF16)
        heads.append(_dot(p, vh).astype(BF16))
    o_ref[...] = x + _dot(jnp.concatenate(heads, axis=1), wo_ref[...])


def _xattn(x, g, w_q, kv, w_o, l, seq, n_mem, tm):
    n, d = x.shape
    tiles_per_seq = seq // tm
    return pl.pallas_call(
        functools.partial(_xattn_kernel, d=d),
        grid=(n // tm,),
        in_specs=[
            pl.BlockSpec((tm, d), lambda i: (i, 0)),
            pl.BlockSpec((1, d), lambda i: (0, 0)),
            _resident(w_q, l),
            pl.BlockSpec((n_mem, 2 * d), lambda i: (i // tiles_per_seq, 0)),
            _resident(w_o, l),
        ],
        out_specs=pl.BlockSpec((tm, d), lambda i: (i, 0)),
        out_shape=jax.ShapeDtypeStruct((n, d), F32),
        compiler_params=_params("arbitrary"),
        name="xattn",
    )(x, g, w_q, kv, w_o)


def _pack_pairs(v):
    c = v.shape[1] // 2
    bits = lax.bitcast_convert_type(v.astype(BF16).astype(F32), jnp.uint32)
    return (bits[:, :c] >> 16) | bits[:, c:]


def _unpack_pairs(w):
    lo = lax.bitcast_convert_type(w << 16, F32)
    hi = lax.bitcast_convert_type(w & jnp.uint32(0xFFFF0000), F32)
    return lo, hi


ROW_E1, ROW_E2, ROW_R1_HI, ROW_R1_LO, ROW_R2_HI, ROW_R2_LO = range(6)
ROUTE_ROWS = 8
RANK_RADIX = 256


def _route_kernel(x_ref, g_ref, wr_ref, br_ref, hp_ref, r_ref, rows_ref, cnt_ref, carry_scr, *, tm):
    @pl.when(pl.program_id(0) == 0)
    def _():
        carry_scr[...] = jnp.zeros_like(carry_scr)

    h = _rms(x_ref[...], g_ref[...])
    hp_ref[...] = _pack_pairs(h)
    h_hi = h.astype(BF16)
    h_lo = (h - h_hi.astype(F32)).astype(BF16)
    logits = _dot(jnp.concatenate([h_hi, h_lo, h_hi], axis=1), wr_ref[...]) + br_ref[...]
    lane = lax.broadcasted_iota(jnp.int32, logits.shape, 1)
    neg = jnp.float32(-jnp.inf)
    big = jnp.int32(ROUTE_LANES)

    is_g = (lane >= N_EXPERTS) & (lane < N_EXPERTS + N_GROUPS)
    lg = jnp.where(is_g, logits, neg)
    gmax = jnp.max(lg, axis=-1, keepdims=True)
    g_sel = jnp.min(jnp.where(lg == gmax, lane - N_EXPERTS, big), axis=-1, keepdims=True)
    p_group = 1.0 / jnp.sum(jnp.exp(lg - gmax), axis=-1, keepdims=True)

    in_group = (lane >= g_sel * EXPERTS_PER_GROUP) & (lane < (g_sel + 1) * EXPERTS_PER_GROUP)
    le = jnp.where(in_group, logits, neg)
    v1 = jnp.max(le, axis=-1, keepdims=True)
    i1 = jnp.min(jnp.where(le == v1, lane, big), axis=-1, keepdims=True)
    le2 = jnp.where(lane == i1, neg, le)
    v2 = jnp.max(le2, axis=-1, keepdims=True)
    i2 = jnp.min(jnp.where(le2 == v2, lane, big), axis=-1, keepdims=True)
    t = jnp.exp(v2 - v1)
    g1 = p_group / (1.0 + t)
    g2 = p_group * t / (1.0 + t)

    oh1 = lane == i1
    oh2 = lane == i2
    both = jnp.where(oh1, 1.0, 0.0) + jnp.where(oh2, 1.0, 0.0)
    rr = lax.broadcasted_iota(jnp.int32, (tm, tm), 0)
    cc = lax.broadcasted_iota(jnp.int32, (tm, tm), 1)
    lower = jnp.where(cc < rr, 1.0, 0.0).astype(BF16)
    before = _dot(lower, both.astype(BF16)) + carry_scr[...]
    rank1 = jnp.sum(jnp.where(oh1, before, 0.0), axis=-1, keepdims=True)
    rank2 = jnp.sum(jnp.where(oh2, before, 0.0), axis=-1, keepdims=True)
    carry_scr[...] = carry_scr[...] + jnp.sum(both, axis=0, keepdims=True)
    cnt_ref[...] = carry_scr[...]

    out = jnp.where(lane == 2, g1, 0.0)
    out = jnp.where(lane == 3, g2, out)
    r_ref[...] = out

    r1_hi = jnp.floor(rank1 * (1.0 / RANK_RADIX))
    r2_hi = jnp.floor(rank2 * (1.0 / RANK_RADIX))
    vals = jnp.where(lane == ROW_E1, i1.astype(F32), 0.0)
    vals = jnp.where(lane == ROW_E2, i2.astype(F32), vals)
    vals = jnp.where(lane == ROW_R1_HI, r1_hi, vals)
    vals = jnp.where(lane == ROW_R1_LO, rank1 - r1_hi * RANK_RADIX, vals)
    vals = jnp.where(lane == ROW_R2_HI, r2_hi, vals)
    vals = jnp.where(lane == ROW_R2_LO, rank2 - r2_hi * RANK_RADIX, vals)
    pick = jnp.where(lax.broadcasted_iota(jnp.int32, (ROUTE_ROWS, ROUTE_LANES), 0)
                     == lax.broadcasted_iota(jnp.int32, (ROUTE_ROWS, ROUTE_LANES), 1), 1.0, 0.0).astype(BF16)
    rows_ref[...] = lax.dot_general(pick, vals.astype(BF16), (((1,), (1,)), ((), ())),
                                    preferred_element_type=F32)


def _route(x, g, w_r, b_r, tm):
    n, d = x.shape
    return pl.pallas_call(
        functools.partial(_route_kernel, tm=tm),
        grid=(n // tm,),
        in_specs=[
            pl.BlockSpec((tm, d), lambda i: (i, 0)),
            pl.BlockSpec((1, d), lambda i: (0, 0)),
            pl.BlockSpec(w_r.shape, lambda i: (0, 0)),
            pl.BlockSpec((1, ROUTE_LANES), lambda i: (0, 0)),
        ],
        out_specs=[
            pl.BlockSpec((tm, d // 2), lambda i: (i, 0)),
            pl.BlockSpec((tm, ROUTE_LANES), lambda i: (i, 0)),
            pl.BlockSpec((ROUTE_ROWS, tm), lambda i: (0, i)),
            pl.BlockSpec((1, ROUTE_LANES), lambda i: (0, 0)),
        ],
        out_shape=[
            jax.ShapeDtypeStruct((n, d // 2), jnp.uint32),
            jax.ShapeDtypeStruct((n, ROUTE_LANES), F32),
            jax.ShapeDtypeStruct((ROUTE_ROWS, n), F32),
            jax.ShapeDtypeStruct((1, ROUTE_LANES), F32),
        ],
        scratch_shapes=[pltpu.VMEM((1, ROUTE_LANES), F32)],
        compiler_params=_params("arbitrary"),
        name="route",
    )(x, g, w_r, b_r)


def _moe_plan(rows, cnt, tm):
    n = rows.shape[1]
    ri = rows.astype(jnp.int32)
    experts = jnp.arange(N_EXPERTS, dtype=jnp.int32)
    counts = cnt[0, :N_EXPERTS].astype(jnp.int32)
    padded = (counts + tm - 1) // tm * tm
    pends = jnp.sum(jnp.where(experts[None, :] <= experts[:, None], padded[None, :], 0), axis=1)
    pstarts = pends - padded

    def dest(e, hi, lo):
        start = jnp.sum(jnp.where(e[:, None] == experts[None, :], pstarts[None, :], 0), axis=1)
        return start + hi * RANK_RADIX + lo

    dest_flat = jnp.concatenate([dest(ri[ROW_E1], ri[ROW_R1_HI], ri[ROW_R1_LO]),
                                 dest(ri[ROW_E2], ri[ROW_R2_HI], ri[ROW_R2_LO])])
    n_tiles = (2 * n + N_EXPERTS * tm) // tm
    n_used = (pends[-1] // tm).astype(jnp.int32)
    tile_start = jnp.arange(n_tiles, dtype=jnp.int32) * tm
    te = jnp.sum((pends[None, :] <= tile_start[:, None]).astype(jnp.int32), axis=1)
    last_used = jnp.max(jnp.where(counts > 0, experts, 0))
    te = jnp.minimum(te, last_used)
    return dest_flat, counts, pends, te, n_used.reshape(1)


def _dispatch_kernel(dest_ref, cnt_ref, pend_ref, h_ref, xp_hbm, zero_scr, sem, zsem, *, tm, tile, n):
    i = pl.program_id(0)

    @pl.when(i == 0)
    def _():
        zero_scr[...] = jnp.zeros_like(zero_scr)

        def zero_tile(t, carry):
            cp = pltpu.make_async_copy(zero_scr, xp_hbm.at[pl.ds(pl.multiple_of(t * tile, tile), tile)], zsem)
            cp.start()
            cp.wait()
            return carry

        lax.fori_loop(pend_ref[N_EXPERTS - 1] // tile, xp_hbm.shape[0] // tile, zero_tile, 0)
        for e in range(N_EXPERTS):
            @pl.when(cnt_ref[e] > 0)
            def _():
                start = pl.multiple_of(pend_ref[e] - tile, tile)
                pltpu.make_async_copy(zero_scr, xp_hbm.at[pl.ds(start, tile)], zsem).start()
        for e in range(N_EXPERTS):
            @pl.when(cnt_ref[e] > 0)
            def _():
                start = pl.multiple_of(pend_ref[e] - tile, tile)
                pltpu.make_async_copy(zero_scr, xp_hbm.at[pl.ds(start, tile)], zsem).wait()

    base = i * tm

    def row_copy(r, k):
        return pltpu.make_async_copy(h_ref.at[pl.ds(r, 1)], xp_hbm.at[pl.ds(dest_ref[k * n + base + r], 1)], sem)

    copies = [row_copy(r, k) for r in range(tm) for k in range(2)]
    for cp in copies:
        cp.start()
    for cp in copies:
        cp.wait()


def _dispatch(dest_flat, counts, pends, hp, tm, tile):
    n, c = hp.shape
    rows = 2 * n + N_EXPERTS * tile
    grid_spec = pltpu.PrefetchScalarGridSpec(
        num_scalar_prefetch=3,
        grid=(n // tm,),
        in_specs=[pl.BlockSpec((tm, c), lambda i, *_: (i, 0))],
        out_specs=pl.BlockSpec(memory_space=pl.ANY),
        scratch_shapes=[pltpu.VMEM((tile, c), jnp.uint32), pltpu.SemaphoreType.DMA, pltpu.SemaphoreType.DMA],
    )
    return pl.pallas_call(
        functools.partial(_dispatch_kernel, tm=tm, tile=tile, n=n),
        grid_spec=grid_spec,
        out_shape=jax.ShapeDtypeStruct((rows, c), jnp.uint32),
        compiler_params=_params("arbitrary"),
        name="dispatch",
    )(dest_flat, counts, pends, hp)


def _gmm_kernel(te_ref, nu_ref, x_ref, wg_ref, wu_ref, wd_ref, y_ref):
    used = pl.program_id(0) < nu_ref[0]

    @pl.when(used)
    def _():
        lo, hi = _unpack_pairs(x_ref[...])
        x = jnp.concatenate([lo.astype(BF16), hi.astype(BF16)], axis=1)
        a = _dot(x, wg_ref[...])
        b = _dot(x, wu_ref[...])
        hid = (a * jax.nn.sigmoid(a) * b).astype(BF16)
        y_ref[...] = _pack_pairs(_dot(hid, wd_ref[...]))

    @pl.when(jnp.logical_not(used))
    def _():
        y_ref[...] = jnp.zeros_like(y_ref)


def _gmm(tile_expert, n_used, x_pad, w_gate, w_up, w_down, l, tm):
    rows, c = x_pad.shape
    _, _, d, de = w_gate.shape

    def row_map(i, te, nu):
        return (jnp.minimum(i, jnp.maximum(nu[0] - 1, 0)), 0)

    grid_spec = pltpu.PrefetchScalarGridSpec(
        num_scalar_prefetch=2,
        grid=(rows // tm,),
        in_specs=[
            pl.BlockSpec((tm, c), row_map),
            pl.BlockSpec((None, None, d, de), lambda i, te, nu: (l, te[i], 0, 0)),
            pl.BlockSpec((None, None, d, de), lambda i, te, nu: (l, te[i], 0, 0)),
            pl.BlockSpec((None, None, de, d), lambda i, te, nu: (l, te[i], 0, 0)),
        ],
        out_specs=pl.BlockSpec((tm, c), lambda i, te, nu: (i, 0)),
    )
    return pl.pallas_call(
        _gmm_kernel,
        grid_spec=grid_spec,
        out_shape=jax.ShapeDtypeStruct((rows, c), jnp.uint32),
        compiler_params=_params("arbitrary"),
        name="gmm",
    )(tile_expert, n_used, x_pad, w_gate, w_up, w_down)


def _combine_kernel(dest_ref, x_ref, r_ref, g_ref, yp_hbm, o_ref, ybuf, sem, *, tm, n, final_norm):
    j = pl.program_id(0)
    n_tiles = pl.num_programs(0) - 1
    c = ybuf.shape[-1]

    def row_copy(src_row, slot, r, k):
        return pltpu.make_async_copy(yp_hbm.at[pl.ds(src_row, 1)], ybuf.at[slot, k, pl.ds(r, 1)], sem.at[slot])

    @pl.when(j < n_tiles)
    def _():
        for r in range(tm):
            for k in range(2):
                row_copy(dest_ref[k * n + j * tm + r], j % 2, r, k).start()

    @pl.when(j > 0)
    def _():
        slot = (j - 1) % 2
        for r in range(tm):
            for k in range(2):
                row_copy(0, slot, r, k).wait()
        g1 = r_ref[:, 2:3]
        g2 = r_ref[:, 3:4]
        lo1, hi1 = _unpack_pairs(ybuf[slot, 0])
        lo2, hi2 = _unpack_pairs(ybuf[slot, 1])
        x = x_ref[...]
        out = jnp.concatenate([x[:, :c] + g1 * lo1 + g2 * lo2, x[:, c:] + g1 * hi1 + g2 * hi2], axis=1)
        o_ref[...] = _rms(out, g_ref[...]) if final_norm else out


def _combine(dest_flat, x, route, g, y_pad, tm, final_norm):
    n, d = x.shape
    c = y_pad.shape[1]
    grid_spec = pltpu.PrefetchScalarGridSpec(
        num_scalar_prefetch=1,
        grid=(n // tm + 1,),
        in_specs=[
            pl.BlockSpec((tm, d), lambda j, *_: (jnp.maximum(j - 1, 0), 0)),
            pl.BlockSpec((tm, ROUTE_LANES), lambda j, *_: (jnp.maximum(j - 1, 0), 0)),
            pl.BlockSpec((1, d), lambda j, *_: (0, 0)),
            pl.BlockSpec(memory_space=pl.ANY),
        ],
        out_specs=pl.BlockSpec((tm, d), lambda j, *_: (jnp.maximum(j - 1, 0), 0)),
        scratch_shapes=[pltpu.VMEM((2, 2, tm, c), jnp.uint32), pltpu.SemaphoreType.DMA((2,))],
    )
    return pl.pallas_call(
        functools.partial(_combine_kernel, tm=tm, n=n, final_norm=final_norm),
        grid_spec=grid_spec,
        out_shape=jax.ShapeDtypeStruct((n, d), F32),
        compiler_params=_params("arbitrary"),
        name="combine",
    )(dest_flat, x, route, g, y_pad)


def _tile(n, want):
    return min(n, want)


def kernel(x, mem, w_in, pool_mix, pool_scale, w_pool_out, conv_w, w_conv_out, w_mix_out, g_mix, g_xattn, g_mem, w_q, w_k, w_v, w_o, g_ffn, w_route_group, b_route_group, w_route_expert, b_route_expert, w_gate, w_up, w_down, g_final):
    bsz, seq, d = x.shape
    n_mem = mem.shape[1]
    depth = w_in.shape[0]
    n = bsz * seq
    xf = x.reshape(n, d)
    memf = mem.reshape(bsz * n_mem, d)

    tm_in = _tile(seq, 1024)
    tm_mid = _tile(seq, 256)
    tm_att = _tile(seq, 256)
    tm_route = _tile(seq, 512)
    tm_disp = _tile(seq, 512)
    tm_gmm = 256
    tm_comb = _tile(seq, 256)

    bf = lambda w: w.astype(BF16)
    w_in, pool_mix, w_pool_out, w_conv_out, w_mix_out = map(bf, (w_in, pool_mix, w_pool_out, w_conv_out, w_mix_out))
    w_q, w_k, w_v, w_o, w_gate, w_up, w_down = map(bf, (w_q, w_k, w_v, w_o, w_gate, w_up, w_down))
    pool_scale = pool_scale.reshape(depth, 1, -1)

    pad_lanes = ROUTE_LANES - N_EXPERTS - N_GROUPS
    for l in range(depth):
        row = lambda v: v[l].reshape(1, -1)
        z = _mixer_in(xf, row(g_mix), w_in, l, tm_in, _tile(w_in.shape[2], 1024))
        xf = _mixer_mid(xf, z, pool_mix, pool_scale, w_pool_out, conv_w, w_conv_out, w_mix_out, l, seq, tm_mid)

        kv = _kv_proj(memf, row(g_mem), w_k, w_v, l, _tile(d, 1024))
        xf = _xattn(xf, row(g_xattn), w_q, kv, w_o, l, seq, n_mem, tm_att)

        w_r = jnp.concatenate([w_route_expert[l], w_route_group[l], jnp.zeros((d, pad_lanes), F32)], axis=1)
        w_r_hi = w_r.astype(BF16)
        w_r_lo = (w_r - w_r_hi.astype(F32)).astype(BF16)
        w_r3 = jnp.concatenate([w_r_hi, w_r_hi, w_r_lo], axis=0)
        b_r = jnp.concatenate([b_route_expert[l], b_route_group[l], jnp.zeros((pad_lanes,), F32)]).reshape(1, -1)
        hp, route, rows, cnt = _route(xf, row(g_ffn), w_r3, b_r, tm_route)
        dest_flat, counts, pends, te, n_used = _moe_plan(rows, cnt, tm_gmm)
        x_pad = _dispatch(dest_flat, counts, pends, hp, tm_disp, tm_gmm)
        y_pad = _gmm(te, n_used, x_pad, w_gate, w_up, w_down, l, tm_gmm)
        xf = _combine(dest_flat, xf, route, g_final.reshape(1, -1), y_pad, tm_comb, final_norm=(l == depth - 1))
    return xf.reshape(bsz, seq, d)
```

```python
import functools
import math

import jax
import jax.numpy as jnp
from jax import lax
from jax.experimental import pallas as pl
from jax.experimental.pallas import tpu as pltpu

EPS = 1e-6
POOL_WINDOWS = (2, 4, 8, 16)
CONV_K = 3
XA_HEADS = 4
N_GROUPS = 4
EXPERTS_PER_GROUP = 8
N_EXPERTS = N_GROUPS * EXPERTS_PER_GROUP
HALO = 16
ROUTE_LANES = 128
VMEM_LIMIT = 56 * 1024 * 1024

F32 = jnp.float32
BF16 = jnp.bfloat16


def _params(*sem):
    return pltpu.CompilerParams(dimension_semantics=sem, vmem_limit_bytes=VMEM_LIMIT)


def _resident(stack, l):
    nd = stack.ndim - 1
    return pl.BlockSpec((None,) + stack.shape[1:], lambda *_: (l,) + (0,) * nd, pipeline_mode=pl.Buffered(1))


def _rms(xf, g):
    ms = jnp.mean(xf * xf, axis=-1, keepdims=True)
    return xf * lax.rsqrt(ms + EPS) * g


def _dot(a, b):
    return jnp.dot(a, b, preferred_element_type=F32)


def _mixer_in_kernel(x_ref, g_ref, w_ref, z_ref, h_scr):
    @pl.when(pl.program_id(1) == 0)
    def _():
        h_scr[...] = _rms(x_ref[...], g_ref[...]).astype(BF16)

    z_ref[...] = _dot(h_scr[...], w_ref[...]).astype(z_ref.dtype)


def _mixer_in(x, g, w, l, tm, tn):
    n, d = x.shape
    c = w.shape[2]
    return pl.pallas_call(
        _mixer_in_kernel,
        grid=(n // tm, c // tn),
        in_specs=[
            pl.BlockSpec((tm, d), lambda i, j: (i, 0)),
            pl.BlockSpec((1, d), lambda i, j: (0, 0)),
            pl.BlockSpec((None, d, tn), lambda i, j: (l, 0, j)),
        ],
        out_specs=pl.BlockSpec((tm, tn), lambda i, j: (i, j)),
        out_shape=jax.ShapeDtypeStruct((n, c), BF16),
        scratch_shapes=[pltpu.VMEM((tm, d), BF16)],
        compiler_params=_params("arbitrary", "arbitrary"),
        name="mixer_in",
    )(x, g, w)


def _mixer_mid_kernel(x_ref, z_ref, zh_ref, pm_ref, ps_ref, wpo_ref, cw_ref, wco_ref, wmo_ref, o_ref,
                      *, tiles_per_seq, tm, d):
    pw = d // 2
    gw = pw // len(POOL_WINDOWS)
    t = pl.program_id(0) % tiles_per_seq
    keep = jnp.where(t == 0, 0.0, 1.0).astype(F32)
    pos = lax.broadcasted_iota(jnp.int32, (tm, gw), 0) + (t * tm + 1)

    parts = []
    for gi, w in enumerate(POOL_WINDOWS):
        cols = slice(gi * gw, (gi + 1) * gw)
        u = z_ref[:, cols].astype(F32)
        ext = jnp.concatenate([zh_ref[:, cols].astype(F32) * keep, u], axis=0)
        s = 1
        while s < w:
            ext = ext + pltpu.roll(ext, s, axis=0)
            s *= 2
        cnt = jnp.minimum(pos, w).astype(F32)
        p = (ext[HALO:, :] / cnt - u).astype(BF16)
        m = _dot(p, pm_ref[gi]) * ps_ref[:, cols]
        parts.append(m.astype(BF16))
    br_a = _dot(jnp.concatenate(parts, axis=1), wpo_ref[...])

    uc = z_ref[:, pw:2 * pw].astype(F32)
    bg = z_ref[:, 2 * pw:3 * pw].astype(F32)
    cg = z_ref[:, 3 * pw:4 * pw].astype(F32)
    v = cg * uc
    vh = zh_ref[:, 3 * pw:4 * pw].astype(F32) * zh_ref[:, pw:2 * pw].astype(F32) * keep
    vext = jnp.concatenate([vh, v], axis=0)
    y = cw_ref[CONV_K - 1:CONV_K, :] * v
    for j in range(CONV_K - 1):
        y = y + cw_ref[j:j + 1, :] * pltpu.roll(vext, CONV_K - 1 - j, axis=0)[HALO:, :]
    br_b = _dot((bg * y).astype(BF16), wco_ref[...])

    ga = z_ref[:, 2 * d:3 * d].astype(F32)
    gb = z_ref[:, 3 * d:4 * d].astype(F32)
    merged = jax.nn.sigmoid(ga) * br_a + jax.nn.sigmoid(gb) * br_b
    o_ref[...] = x_ref[...] + _dot(merged.astype(BF16), wmo_ref[...])


def _mixer_mid(x, z, pool_mix, pool_scale, w_pool_out, conv_w, w_conv_out, w_mix_out, l, seq, tm):
    n, d = x.shape
    c = z.shape[1]
    hb = tm // HALO
    kern = functools.partial(_mixer_mid_kernel, tiles_per_seq=seq // tm, tm=tm, d=d)
    return pl.pallas_call(
        kern,
        grid=(n // tm,),
        in_specs=[
            pl.BlockSpec((tm, d), lambda i: (i, 0)),
            pl.BlockSpec((tm, c), lambda i: (i, 0)),
            pl.BlockSpec((HALO, c // 2), lambda i: (jnp.maximum(i * hb - 1, 0), 0)),
            _resident(pool_mix, l),
            _resident(pool_scale, l),
            _resident(w_pool_out, l),
            _resident(conv_w, l),
            _resident(w_conv_out, l),
            _resident(w_mix_out, l),
        ],
        out_specs=pl.BlockSpec((tm, d), lambda i: (i, 0)),
        out_shape=jax.ShapeDtypeStruct((n, d), F32),
        compiler_params=_params("arbitrary"),
        name="mixer_mid",
    )(x, z, z, pool_mix, pool_scale, w_pool_out, conv_w, w_conv_out, w_mix_out)


def _kv_kernel(m_ref, g_ref, wk_ref, wv_ref, o_ref, *, nb):
    mn = _rms(m_ref[...], g_ref[...]).astype(BF16)
    j = pl.program_id(0)

    @pl.when(j < nb)
    def _():
        o_ref[...] = _dot(mn, wk_ref[...]).astype(o_ref.dtype)

    @pl.when(j >= nb)
    def _():
        o_ref[...] = _dot(mn, wv_ref[...]).astype(o_ref.dtype)


def _kv_proj(mem, g, w_k, w_v, l, tn):
    r, d = mem.shape
    nb = w_k.shape[2] // tn
    return pl.pallas_call(
        functools.partial(_kv_kernel, nb=nb),
        grid=(2 * nb,),
        in_specs=[
            pl.BlockSpec((r, d), lambda j: (0, 0)),
            pl.BlockSpec((1, d), lambda j: (0, 0)),
            pl.BlockSpec((None, d, tn), lambda j: (l, 0, jnp.minimum(j, nb - 1))),
            pl.BlockSpec((None, d, tn), lambda j: (l, 0, jnp.maximum(j - nb, 0))),
        ],
        out_specs=pl.BlockSpec((r, tn), lambda j: (0, j)),
        out_shape=jax.ShapeDtypeStruct((r, 2 * nb * tn), BF16),
        compiler_params=_params("arbitrary"),
        name="kv_proj",
    )(mem, g, w_k, w_v)


def _xattn_kernel(x_ref, g_ref, wq_ref, kv_ref, wo_ref, o_ref, *, d):
    hd = d // XA_HEADS
    x = x_ref[...]
    q = _dot(_rms(x, g_ref[...]).astype(BF16), wq_ref[...]).astype(BF16)
    scale = 1.0 / math.sqrt(hd)
    heads = []
    for h in range(XA_HEADS):
        qh = q[:, h * hd:(h + 1) * hd]
        kh = kv_ref[:, h * hd:(h + 1) * hd]
        vh = kv_ref[:, d + h * hd:d + (h + 1) * hd]
        s = lax.dot_general(qh, kh, (((1,), (1,)), ((), ())), preferred_element_type=F32) * scale
        e = jnp.exp(s - jnp.max(s, axis=-1, keepdims=True))
        p = (e / jnp.sum(e, axis=-1, keepdims=True)).astype(BF16)
        heads.append(_dot(p, vh).astype(BF16))
    o_ref[...] = x + _dot(jnp.concatenate(heads, axis=1), wo_ref[...])


def _xattn(x, g, w_q, kv, w_o, l, seq, n_mem, tm):
    n, d = x.shape
    tiles_per_seq = seq // tm
    return pl.pallas_call(
        functools.partial(_xattn_kernel, d=d),
        grid=(n // tm,),
        in_specs=[
            pl.BlockSpec((tm, d), lambda i: (i, 0)),
            pl.BlockSpec((1, d), lambda i: (0, 0)),
            _resident(w_q, l),
            pl.BlockSpec((n_mem, 2 * d), lambda i: (i // tiles_per_seq, 0)),
            _resident(w_o, l),
        ],
        out_specs=pl.BlockSpec((tm, d), lambda i: (i, 0)),
        out_shape=jax.ShapeDtypeStruct((n, d), F32),
        compiler_params=_params("arbitrary"),
        name="xattn",
    )(x, g, w_q, kv, w_o)


def _pack_pairs(v):
    c = v.shape[1] // 2
    bits = lax.bitcast_convert_type(v.astype(BF16).astype(F32), jnp.uint32)
    return (bits[:, :c] >> 16) | bits[:, c:]


def _unpack_pairs(w):
    lo = lax.bitcast_convert_type(w << 16, F32)
    hi = lax.bitcast_convert_type(w & jnp.uint32(0xFFFF0000), F32)
    return lo, hi


ROW_E1, ROW_E2, ROW_R1_HI, ROW_R1_LO, ROW_R2_HI, ROW_R2_LO = range(6)
ROUTE_ROWS = 8
RANK_RADIX = 256


def _route_kernel(x_ref, g_ref, wr_ref, br_ref, hp_ref, r_ref, rows_ref, cnt_ref, carry_scr, *, tm):
    @pl.when(pl.program_id(0) == 0)
    def _():
        carry_scr[...] = jnp.zeros_like(carry_scr)

    h = _rms(x_ref[...], g_ref[...])
    hp_ref[...] = _pack_pairs(h)
    h_hi = h.astype(BF16)
    h_lo = (h - h_hi.astype(F32)).astype(BF16)
    logits = _dot(jnp.concatenate([h_hi, h_lo, h_hi], axis=1), wr_ref[...]) + br_ref[...]
    lane = lax.broadcasted_iota(jnp.int32, logits.shape, 1)
    neg = jnp.float32(-jnp.inf)
    big = jnp.int32(ROUTE_LANES)

    is_g = (lane >= N_EXPERTS) & (lane < N_EXPERTS + N_GROUPS)
    lg = jnp.where(is_g, logits, neg)
    gmax = jnp.max(lg, axis=-1, keepdims=True)
    g_sel = jnp.min(jnp.where(lg == gmax, lane - N_EXPERTS, big), axis=-1, keepdims=True)
    p_group = 1.0 / jnp.sum(jnp.exp(lg - gmax), axis=-1, keepdims=True)

    in_group = (lane >= g_sel * EXPERTS_PER_GROUP) & (lane < (g_sel + 1) * EXPERTS_PER_GROUP)
    le = jnp.where(in_group, logits, neg)
    v1 = jnp.max(le, axis=-1, keepdims=True)
    i1 = jnp.min(jnp.where(le == v1, lane, big), axis=-1, keepdims=True)
    le2 = jnp.where(lane == i1, neg, le)
    v2 = jnp.max(le2, axis=-1, keepdims=True)
    i2 = jnp.min(jnp.where(le2 == v2, lane, big), axis=-1, keepdims=True)
    t = jnp.exp(v2 - v1)
    g1 = p_group / (1.0 + t)
    g2 = p_group * t / (1.0 + t)

    oh1 = lane == i1
    oh2 = lane == i2
    both = jnp.where(oh1, 1.0, 0.0) + jnp.where(oh2, 1.0, 0.0)
    rr = lax.broadcasted_iota(jnp.int32, (tm, tm), 0)
    cc = lax.broadcasted_iota(jnp.int32, (tm, tm), 1)
    lower = jnp.where(cc < rr, 1.0, 0.0).astype(BF16)
    before = _dot(lower, both.astype(BF16)) + carry_scr[...]
    rank1 = jnp.sum(jnp.where(oh1, before, 0.0), axis=-1, keepdims=True)
    rank2 = jnp.sum(jnp.where(oh2, before, 0.0), axis=-1, keepdims=True)
    carry_scr[...] = carry_scr[...] + jnp.sum(both, axis=0, keepdims=True)
    cnt_ref[...] = carry_scr[...]

    out = jnp.where(lane == 2, g1, 0.0)
    out = jnp.where(lane == 3, g2, out)
    r_ref[...] = out

    r1_hi = jnp.floor(rank1 * (1.0 / RANK_RADIX))
    r2_hi = jnp.floor(rank2 * (1.0 / RANK_RADIX))
    vals = jnp.where(lane == ROW_E1, i1.astype(F32), 0.0)
    vals = jnp.where(lane == ROW_E2, i2.astype(F32), vals)
    vals = jnp.where(lane == ROW_R1_HI, r1_hi, vals)
    vals = jnp.where(lane == ROW_R1_LO, rank1 - r1_hi * RANK_RADIX, vals)
    vals = jnp.where(lane == ROW_R2_HI, r2_hi, vals)
    vals = jnp.where(lane == ROW_R2_LO, rank2 - r2_hi * RANK_RADIX, vals)
    pick = jnp.where(lax.broadcasted_iota(jnp.int32, (ROUTE_ROWS, ROUTE_LANES), 0)
                     == lax.broadcasted_iota(jnp.int32, (ROUTE_ROWS, ROUTE_LANES), 1), 1.0, 0.0).astype(BF16)
    rows_ref[...] = lax.dot_general(pick, vals.astype(BF16), (((1,), (1,)), ((), ())),
                                    preferred_element_type=F32)


def _route(x, g, w_r, b_r, tm):
    n, d = x.shape
    return pl.pallas_call(
        functools.partial(_route_kernel, tm=tm),
        grid=(n // tm,),
        in_specs=[
            pl.BlockSpec((tm, d), lambda i: (i, 0)),
            pl.BlockSpec((1, d), lambda i: (0, 0)),
            pl.BlockSpec(w_r.shape, lambda i: (0, 0)),
            pl.BlockSpec((1, ROUTE_LANES), lambda i: (0, 0)),
        ],
        out_specs=[
            pl.BlockSpec((tm, d // 2), lambda i: (i, 0)),
            pl.BlockSpec((tm, ROUTE_LANES), lambda i: (i, 0)),
            pl.BlockSpec((ROUTE_ROWS, tm), lambda i: (0, i)),
            pl.BlockSpec((1, ROUTE_LANES), lambda i: (0, 0)),
        ],
        out_shape=[
            jax.ShapeDtypeStruct((n, d // 2), jnp.uint32),
            jax.ShapeDtypeStruct((n, ROUTE_LANES), F32),
            jax.ShapeDtypeStruct((ROUTE_ROWS, n), F32),
            jax.ShapeDtypeStruct((1, ROUTE_LANES), F32),
        ],
        scratch_shapes=[pltpu.VMEM((1, ROUTE_LANES), F32)],
        compiler_params=_params("arbitrary"),
        name="route",
    )(x, g, w_r, b_r)


def _moe_plan(rows, cnt, tm):
    n = rows.shape[1]
    ri = rows.astype(jnp.int32)
    experts = jnp.arange(N_EXPERTS, dtype=jnp.int32)
    counts = cnt[0, :N_EXPERTS].astype(jnp.int32)
    padded = (counts + tm - 1) // tm * tm
    pends = jnp.sum(jnp.where(experts[None, :] <= experts[:, None], padded[None, :], 0), axis=1)
    pstarts = pends - padded

    def dest(e, hi, lo):
        start = jnp.sum(jnp.where(e[:, None] == experts[None, :], pstarts[None, :], 0), axis=1)
        return start + hi * RANK_RADIX + lo

    dest_flat = jnp.concatenate([dest(ri[ROW_E1], ri[ROW_R1_HI], ri[ROW_R1_LO]),
                                 dest(ri[ROW_E2], ri[ROW_R2_HI], ri[ROW_R2_LO])])
    n_tiles = (2 * n + N_EXPERTS * tm) // tm
    n_used = (pends[-1] // tm).astype(jnp.int32)
    tile_start = jnp.arange(n_tiles, dtype=jnp.int32) * tm
    te = jnp.sum((pends[None, :] <= tile_start[:, None]).astype(jnp.int32), axis=1)
    last_used = jnp.max(jnp.where(counts > 0, experts, 0))
    te = jnp.minimum(te, last_used)
    return dest_flat, counts, pends, te, n_used.reshape(1)


def _dispatch_kernel(dest_ref, cnt_ref, pend_ref, h_ref, xp_hbm, zero_scr, sem, zsem, *, tm, tile, n):
    i = pl.program_id(0)

    @pl.when(i == 0)
    def _():
        zero_scr[...] = jnp.zeros_like(zero_scr)

        def zero_tile(t, carry):
            cp = pltpu.make_async_copy(zero_scr, xp_hbm.at[pl.ds(pl.multiple_of(t * tile, tile), tile)], zsem)
            cp.start()
            cp.wait()
            return carry

        lax.fori_loop(pend_ref[N_EXPERTS - 1] // tile, xp_hbm.shape[0] // tile, zero_tile, 0)
        for e in range(N_EXPERTS):
            @pl.when(cnt_ref[e] > 0)
            def _():
                start = pl.multiple_of(pend_ref[e] - tile, tile)
                pltpu.make_async_copy(zero_scr, xp_hbm.at[pl.ds(start, tile)], zsem).start()
        for e in range(N_EXPERTS):
            @pl.when(cnt_ref[e] > 0)
            def _():
                start = pl.multiple_of(pend_ref[e] - tile, tile)
                pltpu.make_async_copy(zero_scr, xp_hbm.at[pl.ds(start, tile)], zsem).wait()

    base = i * tm

    def row_copy(r, k):
        return pltpu.make_async_copy(h_ref.at[pl.ds(r, 1)], xp_hbm.at[pl.ds(dest_ref[k * n + base + r], 1)], sem)

    copies = [row_copy(r, k) for r in range(tm) for k in range(2)]
    for cp in copies:
        cp.start()
    for cp in copies:
        cp.wait()


def _dispatch(dest_flat, counts, pends, hp, tm, tile):
    n, c = hp.shape
    rows = 2 * n + N_EXPERTS * tile
    grid_spec = pltpu.PrefetchScalarGridSpec(
        num_scalar_prefetch=3,
        grid=(n // tm,),
        in_specs=[pl.BlockSpec((tm, c), lambda i, *_: (i, 0))],
        out_specs=pl.BlockSpec(memory_space=pl.ANY),
        scratch_shapes=[pltpu.VMEM((tile, c), jnp.uint32), pltpu.SemaphoreType.DMA, pltpu.SemaphoreType.DMA],
    )
    return pl.pallas_call(
        functools.partial(_dispatch_kernel, tm=tm, tile=tile, n=n),
        grid_spec=grid_spec,
        out_shape=jax.ShapeDtypeStruct((rows, c), jnp.uint32),
        compiler_params=_params("arbitrary"),
        name="dispatch",
    )(dest_flat, counts, pends, hp)


N_WGROUPS = 8
N_WBUF = 2


def _gmm_plan(te, n_used, counts, pends, tm):
    n_tiles = te.shape[0]
    idx = jnp.arange(n_tiles, dtype=jnp.int32)
    experts = jnp.arange(N_EXPERTS, dtype=jnp.int32)
    by_tile = lambda table: jnp.sum(jnp.where(te[:, None] == experts[None, :], table[None, :], 0), axis=1)
    run_len = by_tile((counts + tm - 1) // tm)
    pos = idx - by_tile(pends // tm - (counts + tm - 1) // tm)
    later = (experts[None, :] > experts[:, None]) & (counts[None, :] > 0)
    nxt_e = jnp.min(jnp.where(later, experts[None, :], N_EXPERTS), axis=1)
    nxt = by_tile(jnp.where(nxt_e < N_EXPERTS, nxt_e, -1))
    used_before = jnp.sum(jnp.where((experts[None, :] < experts[:, None]) & (counts[None, :] > 0), 1, 0), axis=1)
    slot = by_tile(used_before % 2)
    safe_len = jnp.maximum(run_len, 1)
    g0 = pos * N_WGROUPS // safe_len
    kk = (pos + 1) * N_WGROUPS // safe_len - g0
    first = (pos == 0).astype(jnp.int32)
    fix = lambda v: jnp.where(idx < n_used, v, 0).astype(jnp.int32)
    return fix(nxt) - (idx >= n_used).astype(jnp.int32), fix(slot), fix(first), fix(g0), fix(kk)


def _gmm_kernel(te_ref, nu_ref, nxt_ref, slot_ref, first_ref, g0_ref, kk_ref, x_ref, wg_hbm, wu_hbm, wd_hbm, y_ref,
                wg_s, wu_s, wd_s, stg_g, stg_u, stg_d, sem, *, l):
    i = pl.program_id(0)
    used = i < nu_ref[0]
    ra = wg_s.shape[1] // N_WGROUPS
    rd = wd_s.shape[1] // N_WGROUPS

    def group_copies(e, g):
        b = g % N_WBUF
        a0 = pl.multiple_of(g * ra, ra)
        d0 = pl.multiple_of(g * rd, rd)
        return (pltpu.make_async_copy(wg_hbm.at[l, e, pl.ds(a0, ra)], stg_g.at[b], sem.at[0, b]),
                pltpu.make_async_copy(wu_hbm.at[l, e, pl.ds(a0, ra)], stg_u.at[b], sem.at[1, b]),
                pltpu.make_async_copy(wd_hbm.at[l, e, pl.ds(d0, rd)], stg_d.at[b], sem.at[2, b]))

    def start_group(e, g):
        for cp in group_copies(e, g):
            cp.start()

    def begin_load(e):
        for g in range(N_WBUF):
            start_group(e, g)

    def cast_groups(e, slot, first_g, n_g):
        def body(it, carry):
            g = first_g + it
            b = g % N_WBUF
            for cp in group_copies(e, g):
                cp.wait()
            a0 = pl.multiple_of(g * ra, ra)
            d0 = pl.multiple_of(g * rd, rd)
            wg_s[slot, pl.ds(a0, ra), :] = stg_g[b].astype(BF16)
            wu_s[slot, pl.ds(a0, ra), :] = stg_u[b].astype(BF16)
            wd_s[slot, pl.ds(d0, rd), :] = stg_d[b].astype(BF16)

            @pl.when(g + N_WBUF < N_WGROUPS)
            def _():
                start_group(e, g + N_WBUF)
            return carry
        lax.fori_loop(0, n_g, body, 0)

    @pl.when(used)
    def _():
        slot = slot_ref[i]

        @pl.when(i == 0)
        def _():
            begin_load(te_ref[0])
            cast_groups(te_ref[0], slot, 0, N_WGROUPS)

        nxt = nxt_ref[i]

        @pl.when(nxt >= 0)
        def _():
            @pl.when(first_ref[i] == 1)
            def _():
                begin_load(nxt)
            cast_groups(nxt, 1 - slot, g0_ref[i], kk_ref[i])

        lo, hi = _unpack_pairs(x_ref[...])
        x = jnp.concatenate([lo.astype(BF16), hi.astype(BF16)], axis=1)
        a = _dot(x, wg_s[slot])
        b = _dot(x, wu_s[slot])
        hid = (a * jax.nn.sigmoid(a) * b).astype(BF16)
        y_ref[...] = _pack_pairs(_dot(hid, wd_s[slot]))

    @pl.when(jnp.logical_not(used))
    def _():
        y_ref[...] = jnp.zeros_like(y_ref)


def _gmm(tile_expert, n_used, plan, x_pad, w_gate, w_up, w_down, l, tm):
    rows, c = x_pad.shape
    _, _, d, de = w_gate.shape

    def row_map(i, te, nu, *_):
        return (jnp.minimum(i, jnp.maximum(nu[0] - 1, 0)), 0)

    grid_spec = pltpu.PrefetchScalarGridSpec(
        num_scalar_prefetch=2 + len(plan),
        grid=(rows // tm,),
        in_specs=[
            pl.BlockSpec((tm, c), row_map),
            pl.BlockSpec(memory_space=pl.ANY),
            pl.BlockSpec(memory_space=pl.ANY),
            pl.BlockSpec(memory_space=pl.ANY),
        ],
        out_specs=pl.BlockSpec((tm, c), lambda i, *_: (i, 0)),
        scratch_shapes=[
            pltpu.VMEM((2, d, de), BF16),
            pltpu.VMEM((2, d, de), BF16),
            pltpu.VMEM((2, de, d), BF16),
            pltpu.VMEM((N_WBUF, d // N_WGROUPS, de), F32),
            pltpu.VMEM((N_WBUF, d // N_WGROUPS, de), F32),
            pltpu.VMEM((N_WBUF, de // N_WGROUPS, d), F32),
            pltpu.SemaphoreType.DMA((3, N_WBUF)),
        ],
    )
    return pl.pallas_call(
        functools.partial(_gmm_kernel, l=l),
        grid_spec=grid_spec,
        out_shape=jax.ShapeDtypeStruct((rows, c), jnp.uint32),
        compiler_params=_params("arbitrary"),
        name="gmm",
    )(tile_expert, n_used, *plan, x_pad, w_gate, w_up, w_down)


def _combine_kernel(dest_ref, x_ref, r_ref, g_ref, yp_hbm, o_ref, ybuf, sem, *, tm, n, final_norm):
    j = pl.program_id(0)
    n_tiles = pl.num_programs(0) - 1
    c = ybuf.shape[-1]

    def row_copy(src_row, slot, r, k):
        return pltpu.make_async_copy(yp_hbm.at[pl.ds(src_row, 1)], ybuf.at[slot, k, pl.ds(r, 1)], sem.at[slot])

    @pl.when(j < n_tiles)
    def _():
        for r in range(tm):
            for k in range(2):
                row_copy(dest_ref[k * n + j * tm + r], j % 2, r, k).start()

    @pl.when(j > 0)
    def _():
        slot = (j - 1) % 2
        for r in range(tm):
            for k in range(2):
                row_copy(0, slot, r, k).wait()
        g1 = r_ref[:, 2:3]
        g2 = r_ref[:, 3:4]
        lo1, hi1 = _unpack_pairs(ybuf[slot, 0])
        lo2, hi2 = _unpack_pairs(ybuf[slot, 1])
        x = x_ref[...]
        out = jnp.concatenate([x[:, :c] + g1 * lo1 + g2 * lo2, x[:, c:] + g1 * hi1 + g2 * hi2], axis=1)
        o_ref[...] = _rms(out, g_ref[...]) if final_norm else out


def _combine(dest_flat, x, route, g, y_pad, tm, final_norm):
    n, d = x.shape
    c = y_pad.shape[1]
    grid_spec = pltpu.PrefetchScalarGridSpec(
        num_scalar_prefetch=1,
        grid=(n // tm + 1,),
        in_specs=[
            pl.BlockSpec((tm, d), lambda j, *_: (jnp.maximum(j - 1, 0), 0)),
            pl.BlockSpec((tm, ROUTE_LANES), lambda j, *_: (jnp.maximum(j - 1, 0), 0)),
            pl.BlockSpec((1, d), lambda j, *_: (0, 0)),
            pl.BlockSpec(memory_space=pl.ANY),
        ],
        out_specs=pl.BlockSpec((tm, d), lambda j, *_: (jnp.maximum(j - 1, 0), 0)),
        scratch_shapes=[pltpu.VMEM((2, 2, tm, c), jnp.uint32), pltpu.SemaphoreType.DMA((2,))],
    )
    return pl.pallas_call(
        functools.partial(_combine_kernel, tm=tm, n=n, final_norm=final_norm),
        grid_spec=grid_spec,
        out_shape=jax.ShapeDtypeStruct((n, d), F32),
        compiler_params=_params("arbitrary"),
        name="combine",
    )(dest_flat, x, route, g, y_pad)


def _tile(n, want):
    return min(n, want)


def kernel(x, mem, w_in, pool_mix, pool_scale, w_pool_out, conv_w, w_conv_out, w_mix_out, g_mix, g_xattn, g_mem, w_q, w_k, w_v, w_o, g_ffn, w_route_group, b_route_group, w_route_expert, b_route_expert, w_gate, w_up, w_down, g_final):
    bsz, seq, d = x.shape
    n_mem = mem.shape[1]
    depth = w_in.shape[0]
    n = bsz * seq
    xf = x.reshape(n, d)
    memf = mem.reshape(bsz * n_mem, d)

    tm_in = _tile(seq, 1024)
    tm_mid = _tile(seq, 256)
    tm_att = _tile(seq, 256)
    tm_route = _tile(seq, 512)
    tm_disp = _tile(seq, 512)
    tm_gmm = 256
    tm_comb = _tile(seq, 256)

    bf = lambda w: w.astype(BF16)
    w_in, pool_mix, w_pool_out, w_conv_out, w_mix_out = map(bf, (w_in, pool_mix, w_pool_out, w_conv_out, w_mix_out))
    w_q, w_k, w_v, w_o = map(bf, (w_q, w_k, w_v, w_o))
    pool_scale = pool_scale.reshape(depth, 1, -1)

    pad_lanes = ROUTE_LANES - N_EXPERTS - N_GROUPS
    for l in range(depth):
        row = lambda v: v[l].reshape(1, -1)
        z = _mixer_in(xf, row(g_mix), w_in, l, tm_in, _tile(w_in.shape[2], 1024))
        xf = _mixer_mid(xf, z, pool_mix, pool_scale, w_pool_out, conv_w, w_conv_out, w_mix_out, l, seq, tm_mid)

        kv = _kv_proj(memf, row(g_mem), w_k, w_v, l, _tile(d, 1024))
        xf = _xattn(xf, row(g_xattn), w_q, kv, w_o, l, seq, n_mem, tm_att)

        w_r = jnp.concatenate([w_route_expert[l], w_route_group[l], jnp.zeros((d, pad_lanes), F32)], axis=1)
        w_r_hi = w_r.astype(BF16)
        w_r_lo = (w_r - w_r_hi.astype(F32)).astype(BF16)
        w_r3 = jnp.concatenate([w_r_hi, w_r_hi, w_r_lo], axis=0)
        b_r = jnp.concatenate([b_route_expert[l], b_route_group[l], jnp.zeros((pad_lanes,), F32)]).reshape(1, -1)
        hp, route, rows, cnt = _route(xf, row(g_ffn), w_r3, b_r, tm_route)
        dest_flat, counts, pends, te, n_used = _moe_plan(rows, cnt, tm_gmm)
        x_pad = _dispatch(dest_flat, counts, pends, hp, tm_disp, tm_gmm)
        plan = _gmm_plan(te, n_used, counts, pends, tm_gmm)
        y_pad = _gmm(te, n_used, plan, x_pad, w_gate, w_up, w_down, l, tm_gmm)
        xf = _combine(dest_flat, xf, route, g_final.reshape(1, -1), y_pad, tm_comb, final_norm=(l == depth - 1))
    return xf.reshape(bsz, seq, d)
```

```python
import functools
import math

import jax
import jax.numpy as jnp
from jax import lax
from jax.experimental import pallas as pl
from jax.experimental.pallas import tpu as pltpu

EPS = 1e-6
POOL_WINDOWS = (2, 4, 8, 16)
CONV_K = 3
XA_HEADS = 4
N_GROUPS = 4
EXPERTS_PER_GROUP = 8
N_EXPERTS = N_GROUPS * EXPERTS_PER_GROUP
HALO = 16
ROW_DMA_QUEUES = 2
ROUTE_LANES = 128
VMEM_LIMIT = 56 * 1024 * 1024

F32 = jnp.float32
BF16 = jnp.bfloat16


def _params(*sem):
    return pltpu.CompilerParams(dimension_semantics=sem, vmem_limit_bytes=VMEM_LIMIT)


def _resident(stack, l):
    nd = stack.ndim - 1
    return pl.BlockSpec((None,) + stack.shape[1:], lambda *_: (l,) + (0,) * nd, pipeline_mode=pl.Buffered(1))


def _rms(xf, g):
    ms = jnp.mean(xf * xf, axis=-1, keepdims=True)
    return xf * lax.rsqrt(ms + EPS) * g


def _dot(a, b):
    return jnp.dot(a, b, preferred_element_type=F32)


def _mixer_in_kernel(x_ref, g_ref, w_ref, z_ref, h_scr):
    @pl.when(pl.program_id(1) == 0)
    def _():
        h_scr[...] = _rms(x_ref[...], g_ref[...]).astype(BF16)

    z_ref[...] = _dot(h_scr[...], w_ref[...]).astype(z_ref.dtype)


def _mixer_in(x, g, w, l, tm, tn):
    n, d = x.shape
    c = w.shape[2]
    return pl.pallas_call(
        _mixer_in_kernel,
        grid=(n // tm, c // tn),
        in_specs=[
            pl.BlockSpec((tm, d), lambda i, j: (i, 0)),
            pl.BlockSpec((1, d), lambda i, j: (0, 0)),
            pl.BlockSpec((None, d, tn), lambda i, j: (l, 0, j)),
        ],
        out_specs=pl.BlockSpec((tm, tn), lambda i, j: (i, j)),
        out_shape=jax.ShapeDtypeStruct((n, c), BF16),
        scratch_shapes=[pltpu.VMEM((tm, d), BF16)],
        compiler_params=_params("arbitrary", "arbitrary"),
        name="mixer_in",
    )(x, g, w)


def _mixer_mid_kernel(x_ref, z_ref, zh_ref, pm_ref, ps_ref, wpo_ref, cw_ref, wco_ref, wmo_ref, o_ref,
                      *, tiles_per_seq, tm, d):
    pw = d // 2
    gw = pw // len(POOL_WINDOWS)
    t = pl.program_id(0) % tiles_per_seq
    keep = jnp.where(t == 0, 0.0, 1.0).astype(F32)
    pos = lax.broadcasted_iota(jnp.int32, (tm, gw), 0) + (t * tm + 1)

    parts = []
    for gi, w in enumerate(POOL_WINDOWS):
        cols = slice(gi * gw, (gi + 1) * gw)
        u = z_ref[:, cols].astype(F32)
        ext = jnp.concatenate([zh_ref[:, cols].astype(F32) * keep, u], axis=0)
        s = 1
        while s < w:
            ext = ext + pltpu.roll(ext, s, axis=0)
            s *= 2
        cnt = jnp.minimum(pos, w).astype(F32)
        p = (ext[HALO:, :] / cnt - u).astype(BF16)
        m = _dot(p, pm_ref[gi]) * ps_ref[:, cols]
        parts.append(m.astype(BF16))
    br_a = _dot(jnp.concatenate(parts, axis=1), wpo_ref[...])

    uc = z_ref[:, pw:2 * pw].astype(F32)
    bg = z_ref[:, 2 * pw:3 * pw].astype(F32)
    cg = z_ref[:, 3 * pw:4 * pw].astype(F32)
    v = cg * uc
    vh = zh_ref[:, 3 * pw:4 * pw].astype(F32) * zh_ref[:, pw:2 * pw].astype(F32) * keep
    vext = jnp.concatenate([vh, v], axis=0)
    y = cw_ref[CONV_K - 1:CONV_K, :] * v
    for j in range(CONV_K - 1):
        y = y + cw_ref[j:j + 1, :] * pltpu.roll(vext, CONV_K - 1 - j, axis=0)[HALO:, :]
    br_b = _dot((bg * y).astype(BF16), wco_ref[...])

    ga = z_ref[:, 2 * d:3 * d].astype(F32)
    gb = z_ref[:, 3 * d:4 * d].astype(F32)
    merged = jax.nn.sigmoid(ga) * br_a + jax.nn.sigmoid(gb) * br_b
    o_ref[...] = x_ref[...] + _dot(merged.astype(BF16), wmo_ref[...])


def _mixer_mid(x, z, pool_mix, pool_scale, w_pool_out, conv_w, w_conv_out, w_mix_out, l, seq, tm):
    n, d = x.shape
    c = z.shape[1]
    hb = tm // HALO
    kern = functools.partial(_mixer_mid_kernel, tiles_per_seq=seq // tm, tm=tm, d=d)
    return pl.pallas_call(
        kern,
        grid=(n // tm,),
        in_specs=[
            pl.BlockSpec((tm, d), lambda i: (i, 0)),
            pl.BlockSpec((tm, c), lambda i: (i, 0)),
            pl.BlockSpec((HALO, c // 2), lambda i: (jnp.maximum(i * hb - 1, 0), 0)),
            _resident(pool_mix, l),
            _resident(pool_scale, l),
            _resident(w_pool_out, l),
            _resident(conv_w, l),
            _resident(w_conv_out, l),
            _resident(w_mix_out, l),
        ],
        out_specs=pl.BlockSpec((tm, d), lambda i: (i, 0)),
        out_shape=jax.ShapeDtypeStruct((n, d), F32),
        compiler_params=_params("arbitrary"),
        name="mixer_mid",
    )(x, z, z, pool_mix, pool_scale, w_pool_out, conv_w, w_conv_out, w_mix_out)


def _kv_kernel(m_ref, g_ref, wk_ref, wv_ref, o_ref, *, nb):
    mn = _rms(m_ref[...], g_ref[...]).astype(BF16)
    j = pl.program_id(0)

    @pl.when(j < nb)
    def _():
        o_ref[...] = _dot(mn, wk_ref[...]).astype(o_ref.dtype)

    @pl.when(j >= nb)
    def _():
        o_ref[...] = _dot(mn, wv_ref[...]).astype(o_ref.dtype)


def _kv_proj(mem, g, w_k, w_v, l, tn):
    r, d = mem.shape
    nb = w_k.shape[2] // tn
    return pl.pallas_call(
        functools.partial(_kv_kernel, nb=nb),
        grid=(2 * nb,),
        in_specs=[
            pl.BlockSpec((r, d), lambda j: (0, 0)),
            pl.BlockSpec((1, d), lambda j: (0, 0)),
            pl.BlockSpec((None, d, tn), lambda j: (l, 0, jnp.minimum(j, nb - 1))),
            pl.BlockSpec((None, d, tn), lambda j: (l, 0, jnp.maximum(j - nb, 0))),
        ],
        out_specs=pl.BlockSpec((r, tn), lambda j: (0, j)),
        out_shape=jax.ShapeDtypeStruct((r, 2 * nb * tn), BF16),
        compiler_params=_params("arbitrary"),
        name="kv_proj",
    )(mem, g, w_k, w_v)


XA_SUBTILES = 1


def _xattn_kernel(x_ref, g_ref, wq_ref, kv_ref, wo_ref, o_ref, *, d):
    hd = d // XA_HEADS
    scale = 1.0 / math.sqrt(hd)
    rows = x_ref.shape[0] // XA_SUBTILES
    for sub in range(XA_SUBTILES):
        rs = slice(sub * rows, (sub + 1) * rows)
        x = x_ref[rs, :]
        q = _dot(_rms(x, g_ref[...]).astype(BF16), wq_ref[...]).astype(BF16)
        heads = []
        for h in range(XA_HEADS):
            qh = q[:, h * hd:(h + 1) * hd]
            kh = kv_ref[:, h * hd:(h + 1) * hd]
            vh = kv_ref[:, d + h * hd:d + (h + 1) * hd]
            s = lax.dot_general(qh, kh, (((1,), (1,)), ((), ())), preferred_element_type=F32) * scale
            e = jnp.exp(s - jnp.max(s, axis=-1, keepdims=True))
            p = (e / jnp.sum(e, axis=-1, keepdims=True)).astype(BF16)
            heads.append(_dot(p, vh).astype(BF16))
        o_ref[rs, :] = x + _dot(jnp.concatenate(heads, axis=1), wo_ref[...])


def _xattn(x, g, w_q, kv, w_o, l, seq, n_mem, tm):
    n, d = x.shape
    tiles_per_seq = seq // tm
    return pl.pallas_call(
        functools.partial(_xattn_kernel, d=d),
        grid=(n // tm,),
        in_specs=[
            pl.BlockSpec((tm, d), lambda i: (i, 0)),
            pl.BlockSpec((1, d), lambda i: (0, 0)),
            _resident(w_q, l),
            pl.BlockSpec((n_mem, 2 * d), lambda i: (i // tiles_per_seq, 0)),
            _resident(w_o, l),
        ],
        out_specs=pl.BlockSpec((tm, d), lambda i: (i, 0)),
        out_shape=jax.ShapeDtypeStruct((n, d), F32),
        compiler_params=_params("arbitrary"),
        name="xattn",
    )(x, g, w_q, kv, w_o)


def _pack_pairs(v):
    c = v.shape[1] // 2
    bits = lax.bitcast_convert_type(v.astype(BF16).astype(F32), jnp.uint32)
    return (bits[:, :c] >> 16) | bits[:, c:]


def _unpack_pairs(w):
    lo = lax.bitcast_convert_type(w << 16, F32)
    hi = lax.bitcast_convert_type(w & jnp.uint32(0xFFFF0000), F32)
    return lo, hi


ROW_E1, ROW_E2, ROW_R1_HI, ROW_R1_LO, ROW_R2_HI, ROW_R2_LO = range(6)
ROUTE_ROWS = 8
RANK_RADIX = 256


def _route_rows(x, g, wr, br, carry):
    tm = x.shape[0]
    h = _rms(x, g)
    h_hi = h.astype(BF16)
    h_lo = (h - h_hi.astype(F32)).astype(BF16)
    logits = _dot(jnp.concatenate([h_hi, h_lo, h_hi], axis=1), wr) + br
    lane = lax.broadcasted_iota(jnp.int32, logits.shape, 1)
    neg = jnp.float32(-jnp.inf)
    big = jnp.int32(ROUTE_LANES)

    is_g = (lane >= N_EXPERTS) & (lane < N_EXPERTS + N_GROUPS)
    lg = jnp.where(is_g, logits, neg)
    gmax = jnp.max(lg, axis=-1, keepdims=True)
    g_sel = jnp.min(jnp.where(lg == gmax, lane - N_EXPERTS, big), axis=-1, keepdims=True)
    p_group = 1.0 / jnp.sum(jnp.exp(lg - gmax), axis=-1, keepdims=True)

    in_group = (lane >= g_sel * EXPERTS_PER_GROUP) & (lane < (g_sel + 1) * EXPERTS_PER_GROUP)
    le = jnp.where(in_group, logits, neg)
    v1 = jnp.max(le, axis=-1, keepdims=True)
    i1 = jnp.min(jnp.where(le == v1, lane, big), axis=-1, keepdims=True)
    le2 = jnp.where(lane == i1, neg, le)
    v2 = jnp.max(le2, axis=-1, keepdims=True)
    i2 = jnp.min(jnp.where(le2 == v2, lane, big), axis=-1, keepdims=True)
    t = jnp.exp(v2 - v1)
    g1 = p_group / (1.0 + t)
    g2 = p_group * t / (1.0 + t)

    oh1 = lane == i1
    oh2 = lane == i2
    both = jnp.where(oh1, 1.0, 0.0) + jnp.where(oh2, 1.0, 0.0)
    rr = lax.broadcasted_iota(jnp.int32, (tm, tm), 0)
    cc = lax.broadcasted_iota(jnp.int32, (tm, tm), 1)
    lower = jnp.where(cc < rr, 1.0, 0.0).astype(BF16)
    before = _dot(lower, both.astype(BF16)) + carry
    rank1 = jnp.sum(jnp.where(oh1, before, 0.0), axis=-1, keepdims=True)
    rank2 = jnp.sum(jnp.where(oh2, before, 0.0), axis=-1, keepdims=True)

    gates = jnp.where(lane == 2, g1, 0.0)
    gates = jnp.where(lane == 3, g2, gates)

    r1_hi = jnp.floor(rank1 * (1.0 / RANK_RADIX))
    r2_hi = jnp.floor(rank2 * (1.0 / RANK_RADIX))
    vals = jnp.where(lane == ROW_E1, i1.astype(F32), 0.0)
    vals = jnp.where(lane == ROW_E2, i2.astype(F32), vals)
    vals = jnp.where(lane == ROW_R1_HI, r1_hi, vals)
    vals = jnp.where(lane == ROW_R1_LO, rank1 - r1_hi * RANK_RADIX, vals)
    vals = jnp.where(lane == ROW_R2_HI, r2_hi, vals)
    vals = jnp.where(lane == ROW_R2_LO, rank2 - r2_hi * RANK_RADIX, vals)
    pick = jnp.where(lax.broadcasted_iota(jnp.int32, (ROUTE_ROWS, ROUTE_LANES), 0)
                     == lax.broadcasted_iota(jnp.int32, (ROUTE_ROWS, ROUTE_LANES), 1), 1.0, 0.0).astype(BF16)
    rows = lax.dot_general(pick, vals.astype(BF16), (((1,), (1,)), ((), ())), preferred_element_type=F32)
    return _pack_pairs(h), gates, rows, carry + jnp.sum(both, axis=0, keepdims=True)


ROUTE_SUBTILES = 1


def _route_kernel(x_ref, g_ref, wr_ref, br_ref, hp_ref, r_ref, rows_ref, cnt_ref, carry_scr):
    @pl.when(pl.program_id(0) == 0)
    def _():
        carry_scr[...] = jnp.zeros_like(carry_scr)

    carry = carry_scr[...]
    rows = x_ref.shape[0] // ROUTE_SUBTILES
    for sub in range(ROUTE_SUBTILES):
        rs = slice(sub * rows, (sub + 1) * rows)
        hp, gates, row_form, carry = _route_rows(x_ref[rs, :], g_ref[...], wr_ref[...], br_ref[...], carry)
        hp_ref[rs, :] = hp
        r_ref[rs, :] = gates
        rows_ref[:, rs] = row_form
    carry_scr[...] = carry
    cnt_ref[...] = carry


def _route(x, g, w_r, b_r, tm):
    n, d = x.shape
    return pl.pallas_call(
        _route_kernel,
        grid=(n // tm,),
        in_specs=[
            pl.BlockSpec((tm, d), lambda i: (i, 0)),
            pl.BlockSpec((1, d), lambda i: (0, 0)),
            pl.BlockSpec(w_r.shape, lambda i: (0, 0)),
            pl.BlockSpec((1, ROUTE_LANES), lambda i: (0, 0)),
        ],
        out_specs=[
            pl.BlockSpec((tm, d // 2), lambda i: (i, 0)),
            pl.BlockSpec((tm, ROUTE_LANES), lambda i: (i, 0)),
            pl.BlockSpec((ROUTE_ROWS, tm), lambda i: (0, i)),
            pl.BlockSpec((1, ROUTE_LANES), lambda i: (0, 0)),
        ],
        out_shape=[
            jax.ShapeDtypeStruct((n, d // 2), jnp.uint32),
            jax.ShapeDtypeStruct((n, ROUTE_LANES), F32),
            jax.ShapeDtypeStruct((ROUTE_ROWS, n), F32),
            jax.ShapeDtypeStruct((1, ROUTE_LANES), F32),
        ],
        scratch_shapes=[pltpu.VMEM((1, ROUTE_LANES), F32)],
        compiler_params=_params("arbitrary"),
        name="route",
    )(x, g, w_r, b_r)


def _moe_plan(rows, cnt, tm):
    n = rows.shape[1]
    ri = rows.astype(jnp.int32)
    experts = jnp.arange(N_EXPERTS, dtype=jnp.int32)
    counts = cnt[0, :N_EXPERTS].astype(jnp.int32)
    padded = (counts + tm - 1) // tm * tm
    pends = jnp.sum(jnp.where(experts[None, :] <= experts[:, None], padded[None, :], 0), axis=1)
    pstarts = pends - padded

    def dest(e, hi, lo):
        start = jnp.sum(jnp.where(e[:, None] == experts[None, :], pstarts[None, :], 0), axis=1)
        return start + hi * RANK_RADIX + lo

    dest_flat = jnp.concatenate([dest(ri[ROW_E1], ri[ROW_R1_HI], ri[ROW_R1_LO]),
                                 dest(ri[ROW_E2], ri[ROW_R2_HI], ri[ROW_R2_LO])])
    n_tiles = (2 * n + N_EXPERTS * tm) // tm
    n_used = (pends[-1] // tm).astype(jnp.int32)
    tile_start = jnp.arange(n_tiles, dtype=jnp.int32) * tm
    te = jnp.sum((pends[None, :] <= tile_start[:, None]).astype(jnp.int32), axis=1)
    last_used = jnp.max(jnp.where(counts > 0, experts, 0))
    te = jnp.minimum(te, last_used)
    return dest_flat, counts, pends, te, n_used.reshape(1)


def _dispatch_kernel(dest_ref, cnt_ref, pend_ref, h_ref, xp_hbm, zero_scr, sem, zsem, *, tm, tile, n):
    i = pl.program_id(0)

    @pl.when(i == 0)
    def _():
        zero_scr[...] = jnp.zeros_like(zero_scr)

        def zero_tile(t, carry):
            cp = pltpu.make_async_copy(zero_scr, xp_hbm.at[pl.ds(pl.multiple_of(t * tile, tile), tile)], zsem)
            cp.start()
            cp.wait()
            return carry

        lax.fori_loop(pend_ref[N_EXPERTS - 1] // tile, xp_hbm.shape[0] // tile, zero_tile, 0)
        for e in range(N_EXPERTS):
            @pl.when(cnt_ref[e] > 0)
            def _():
                start = pl.multiple_of(pend_ref[e] - tile, tile)
                pltpu.make_async_copy(zero_scr, xp_hbm.at[pl.ds(start, tile)], zsem).start()
        for e in range(N_EXPERTS):
            @pl.when(cnt_ref[e] > 0)
            def _():
                start = pl.multiple_of(pend_ref[e] - tile, tile)
                pltpu.make_async_copy(zero_scr, xp_hbm.at[pl.ds(start, tile)], zsem).wait()

    base = i * tm

    def row_copy(r, k):
        return pltpu.make_async_copy(h_ref.at[pl.ds(r, 1)], xp_hbm.at[pl.ds(dest_ref[k * n + base + r], 1)], sem)

    copies = [row_copy(r, k) for r in range(tm) for k in range(2)]
    for ci, cp in enumerate(copies):
        cp.start(priority=ci % ROW_DMA_QUEUES)
    for cp in copies:
        cp.wait()


def _dispatch(dest_flat, counts, pends, hp, tm, tile):
    n, c = hp.shape
    rows = 2 * n + N_EXPERTS * tile
    grid_spec = pltpu.PrefetchScalarGridSpec(
        num_scalar_prefetch=3,
        grid=(n // tm,),
        in_specs=[pl.BlockSpec((tm, c), lambda i, *_: (i, 0))],
        out_specs=pl.BlockSpec(memory_space=pl.ANY),
        scratch_shapes=[pltpu.VMEM((tile, c), jnp.uint32), pltpu.SemaphoreType.DMA, pltpu.SemaphoreType.DMA],
    )
    return pl.pallas_call(
        functools.partial(_dispatch_kernel, tm=tm, tile=tile, n=n),
        grid_spec=grid_spec,
        out_shape=jax.ShapeDtypeStruct((rows, c), jnp.uint32),
        compiler_params=_params("arbitrary"),
        name="dispatch",
    )(dest_flat, counts, pends, hp)


N_WGROUPS = 8
N_WBUF = 2


def _gmm_plan(te, n_used, counts, pends, tm):
    n_tiles = te.shape[0]
    idx = jnp.arange(n_tiles, dtype=jnp.int32)
    experts = jnp.arange(N_EXPERTS, dtype=jnp.int32)
    by_tile = lambda table: jnp.sum(jnp.where(te[:, None] == experts[None, :], table[None, :], 0), axis=1)
    run_len = by_tile((counts + tm - 1) // tm)
    pos = idx - by_tile(pends // tm - (counts + tm - 1) // tm)
    later = (experts[None, :] > experts[:, None]) & (counts[None, :] > 0)
    nxt_e = jnp.min(jnp.where(later, experts[None, :], N_EXPERTS), axis=1)
    nxt = by_tile(jnp.where(nxt_e < N_EXPERTS, nxt_e, -1))
    used_before = jnp.sum(jnp.where((experts[None, :] < experts[:, None]) & (counts[None, :] > 0), 1, 0), axis=1)
    slot = by_tile(used_before % 2)
    safe_len = jnp.maximum(run_len, 1)
    g0 = pos * N_WGROUPS // safe_len
    kk = (pos + 1) * N_WGROUPS // safe_len - g0
    first = (pos == 0).astype(jnp.int32)
    fix = lambda v: jnp.where(idx < n_used, v, 0).astype(jnp.int32)
    return fix(nxt) - (idx >= n_used).astype(jnp.int32), fix(slot), fix(first), fix(g0), fix(kk)


def _gmm_kernel(te_ref, nu_ref, nxt_ref, slot_ref, first_ref, g0_ref, kk_ref, x_ref, wg_hbm, wu_hbm, wd_hbm, y_ref,
                wg_s, wu_s, wd_s, stg_g, stg_u, stg_d, sem, *, l):
    i = pl.program_id(0)
    used = i < nu_ref[0]
    ra = wg_s.shape[1] // N_WGROUPS
    rd = wd_s.shape[1] // N_WGROUPS

    def group_copies(e, g):
        b = g % N_WBUF
        a0 = pl.multiple_of(g * ra, ra)
        d0 = pl.multiple_of(g * rd, rd)
        return (pltpu.make_async_copy(wg_hbm.at[l, e, pl.ds(a0, ra)], stg_g.at[b], sem.at[0, b]),
                pltpu.make_async_copy(wu_hbm.at[l, e, pl.ds(a0, ra)], stg_u.at[b], sem.at[1, b]),
                pltpu.make_async_copy(wd_hbm.at[l, e, pl.ds(d0, rd)], stg_d.at[b], sem.at[2, b]))

    def start_group(e, g):
        for cp in group_copies(e, g):
            cp.start()

    def begin_load(e):
        for g in range(N_WBUF):
            start_group(e, g)

    def cast_groups(e, slot, first_g, n_g):
        def body(it, carry):
            g = first_g + it
            b = g % N_WBUF
            for cp in group_copies(e, g):
                cp.wait()
            a0 = pl.multiple_of(g * ra, ra)
            d0 = pl.multiple_of(g * rd, rd)
            wg_s[slot, pl.ds(a0, ra), :] = stg_g[b].astype(BF16)
            wu_s[slot, pl.ds(a0, ra), :] = stg_u[b].astype(BF16)
            wd_s[slot, pl.ds(d0, rd), :] = stg_d[b].astype(BF16)

            @pl.when(g + N_WBUF < N_WGROUPS)
            def _():
                start_group(e, g + N_WBUF)
            return carry
        lax.fori_loop(0, n_g, body, 0)

    @pl.when(used)
    def _():
        slot = slot_ref[i]

        @pl.when(i == 0)
        def _():
            begin_load(te_ref[0])
            cast_groups(te_ref[0], slot, 0, N_WGROUPS)

        nxt = nxt_ref[i]

        @pl.when(nxt >= 0)
        def _():
            @pl.when(first_ref[i] == 1)
            def _():
                begin_load(nxt)
            cast_groups(nxt, 1 - slot, g0_ref[i], kk_ref[i])

        lo, hi = _unpack_pairs(x_ref[...])
        x = jnp.concatenate([lo.astype(BF16), hi.astype(BF16)], axis=1)
        a = _dot(x, wg_s[slot])
        b = _dot(x, wu_s[slot])
        hid = (a * jax.nn.sigmoid(a) * b).astype(BF16)
        y_ref[...] = _pack_pairs(_dot(hid, wd_s[slot]))

    @pl.when(jnp.logical_not(used))
    def _():
        y_ref[...] = jnp.zeros_like(y_ref)


def _gmm(tile_expert, n_used, plan, x_pad, w_gate, w_up, w_down, l, tm):
    rows, c = x_pad.shape
    _, _, d, de = w_gate.shape

    def row_map(i, te, nu, *_):
        return (jnp.minimum(i, jnp.maximum(nu[0] - 1, 0)), 0)

    grid_spec = pltpu.PrefetchScalarGridSpec(
        num_scalar_prefetch=2 + len(plan),
        grid=(rows // tm,),
        in_specs=[
            pl.BlockSpec((tm, c), row_map),
            pl.BlockSpec(memory_space=pl.ANY),
            pl.BlockSpec(memory_space=pl.ANY),
            pl.BlockSpec(memory_space=pl.ANY),
        ],
        out_specs=pl.BlockSpec((tm, c), lambda i, *_: (i, 0)),
        scratch_shapes=[
            pltpu.VMEM((2, d, de), BF16),
            pltpu.VMEM((2, d, de), BF16),
            pltpu.VMEM((2, de, d), BF16),
            pltpu.VMEM((N_WBUF, d // N_WGROUPS, de), F32),
            pltpu.VMEM((N_WBUF, d // N_WGROUPS, de), F32),
            pltpu.VMEM((N_WBUF, de // N_WGROUPS, d), F32),
            pltpu.SemaphoreType.DMA((3, N_WBUF)),
        ],
    )
    return pl.pallas_call(
        functools.partial(_gmm_kernel, l=l),
        grid_spec=grid_spec,
        out_shape=jax.ShapeDtypeStruct((rows, c), jnp.uint32),
        compiler_params=_params("arbitrary"),
        name="gmm",
    )(tile_expert, n_used, *plan, x_pad, w_gate, w_up, w_down)


def _combine_kernel(dest_ref, x_ref, r_ref, g_ref, yp_hbm, o_ref, ybuf, sem, *, tm, n, final_norm):
    j = pl.program_id(0)
    n_tiles = pl.num_programs(0) - 1
    c = ybuf.shape[-1]

    def row_copy(src_row, slot, r, k):
        return pltpu.make_async_copy(yp_hbm.at[pl.ds(src_row, 1)], ybuf.at[slot, k, pl.ds(r, 1)], sem.at[slot])

    @pl.when(j < n_tiles)
    def _():
        for r in range(tm):
            for k in range(2):
                row_copy(dest_ref[k * n + j * tm + r], j % 2, r, k).start(priority=(2 * r + k) % ROW_DMA_QUEUES)

    @pl.when(j > 0)
    def _():
        slot = (j - 1) % 2
        for r in range(tm):
            for k in range(2):
                row_copy(0, slot, r, k).wait()
        g1 = r_ref[:, 2:3]
        g2 = r_ref[:, 3:4]
        lo1, hi1 = _unpack_pairs(ybuf[slot, 0])
        lo2, hi2 = _unpack_pairs(ybuf[slot, 1])
        x = x_ref[...]
        out = jnp.concatenate([x[:, :c] + g1 * lo1 + g2 * lo2, x[:, c:] + g1 * hi1 + g2 * hi2], axis=1)
        o_ref[...] = _rms(out, g_ref[...]) if final_norm else out


def _combine(dest_flat, x, route, g, y_pad, tm, final_norm):
    n, d = x.shape
    c = y_pad.shape[1]
    grid_spec = pltpu.PrefetchScalarGridSpec(
        num_scalar_prefetch=1,
        grid=(n // tm + 1,),
        in_specs=[
            pl.BlockSpec((tm, d), lambda j, *_: (jnp.maximum(j - 1, 0), 0)),
            pl.BlockSpec((tm, ROUTE_LANES), lambda j, *_: (jnp.maximum(j - 1, 0), 0)),
            pl.BlockSpec((1, d), lambda j, *_: (0, 0)),
            pl.BlockSpec(memory_space=pl.ANY),
        ],
        out_specs=pl.BlockSpec((tm, d), lambda j, *_: (jnp.maximum(j - 1, 0), 0)),
        scratch_shapes=[pltpu.VMEM((2, 2, tm, c), jnp.uint32), pltpu.SemaphoreType.DMA((2,))],
    )
    return pl.pallas_call(
        functools.partial(_combine_kernel, tm=tm, n=n, final_norm=final_norm),
        grid_spec=grid_spec,
        out_shape=jax.ShapeDtypeStruct((n, d), F32),
        compiler_params=_params("arbitrary"),
        name="combine",
    )(dest_flat, x, route, g, y_pad)


def _tile(n, want):
    return min(n, want)


def kernel(x, mem, w_in, pool_mix, pool_scale, w_pool_out, conv_w, w_conv_out, w_mix_out, g_mix, g_xattn, g_mem, w_q, w_k, w_v, w_o, g_ffn, w_route_group, b_route_group, w_route_expert, b_route_expert, w_gate, w_up, w_down, g_final):
    bsz, seq, d = x.shape
    n_mem = mem.shape[1]
    depth = w_in.shape[0]
    n = bsz * seq
    xf = x.reshape(n, d)
    memf = mem.reshape(bsz * n_mem, d)

    tm_in = _tile(seq, 1024)
    tm_mid = _tile(seq, 256)
    tm_att = _tile(seq, 512)
    tm_route = _tile(seq, 512)
    tm_disp = _tile(seq, 512)
    tm_gmm = 256
    tm_comb = _tile(seq, 256)

    bf = lambda w: w.astype(BF16)
    w_in, pool_mix, w_pool_out, w_conv_out, w_mix_out = map(bf, (w_in, pool_mix, w_pool_out, w_conv_out, w_mix_out))
    w_q, w_k, w_v, w_o = map(bf, (w_q, w_k, w_v, w_o))
    pool_scale = pool_scale.reshape(depth, 1, -1)

    pad_lanes = ROUTE_LANES - N_EXPERTS - N_GROUPS
    for l in range(depth):
        row = lambda v: v[l].reshape(1, -1)
        z = _mixer_in(xf, row(g_mix), w_in, l, tm_in, _tile(w_in.shape[2], 1024))
        xf = _mixer_mid(xf, z, pool_mix, pool_scale, w_pool_out, conv_w, w_conv_out, w_mix_out, l, seq, tm_mid)

        kv = _kv_proj(memf, row(g_mem), w_k, w_v, l, _tile(d, 1024))
        xf = _xattn(xf, row(g_xattn), w_q, kv, w_o, l, seq, n_mem, tm_att)

        w_r = jnp.concatenate([w_route_expert[l], w_route_group[l], jnp.zeros((d, pad_lanes), F32)], axis=1)
        w_r_hi = w_r.astype(BF16)
        w_r_lo = (w_r - w_r_hi.astype(F32)).astype(BF16)
        w_r3 = jnp.concatenate([w_r_hi, w_r_hi, w_r_lo], axis=0)
        b_r = jnp.concatenate([b_route_expert[l], b_route_group[l], jnp.zeros((pad_lanes,), F32)]).reshape(1, -1)
        hp, route, rows, cnt = _route(xf, row(g_ffn), w_r3, b_r, tm_route)
        dest_flat, counts, pends, te, n_used = _moe_plan(rows, cnt, tm_gmm)
        x_pad = _dispatch(dest_flat, counts, pends, hp, tm_disp, tm_gmm)
        plan = _gmm_plan(te, n_used, counts, pends, tm_gmm)
        y_pad = _gmm(te, n_used, plan, x_pad, w_gate, w_up, w_down, l, tm_gmm)
        xf = _combine(dest_flat, xf, route, g_final.reshape(1, -1), y_pad, tm_comb, final_norm=(l == depth - 1))
    return xf.reshape(bsz, seq, d)
```

```python
import functools
import math

import jax
import jax.numpy as jnp
from jax import lax
from jax.experimental import pallas as pl
from jax.experimental.pallas import tpu as pltpu

EPS = 1e-6
POOL_WINDOWS = (2, 4, 8, 16)
CONV_K = 3
XA_HEADS = 4
N_GROUPS = 4
EXPERTS_PER_GROUP = 8
N_EXPERTS = N_GROUPS * EXPERTS_PER_GROUP
HALO = 16
ROW_DMA_QUEUES = 2
ROUTE_LANES = 128
VMEM_LIMIT = 56 * 1024 * 1024

F32 = jnp.float32
BF16 = jnp.bfloat16


def _params(*sem):
    return pltpu.CompilerParams(dimension_semantics=sem, vmem_limit_bytes=VMEM_LIMIT)


def _resident(stack, l):
    nd = stack.ndim - 1
    return pl.BlockSpec((None,) + stack.shape[1:], lambda *_: (l,) + (0,) * nd, pipeline_mode=pl.Buffered(1))


def _rms(xf, g):
    ms = jnp.mean(xf * xf, axis=-1, keepdims=True)
    return xf * lax.rsqrt(ms + EPS) * g


def _dot(a, b):
    return jnp.dot(a, b, preferred_element_type=F32)


def _mixer_in_kernel(x_ref, g_ref, w_ref, z_ref, h_scr):
    @pl.when(pl.program_id(1) == 0)
    def _():
        h_scr[...] = _rms(x_ref[...], g_ref[...]).astype(BF16)

    z_ref[...] = _dot(h_scr[...], w_ref[...]).astype(z_ref.dtype)


def _mixer_in(x, g, w, l, tm, tn):
    n, d = x.shape
    c = w.shape[2]
    return pl.pallas_call(
        _mixer_in_kernel,
        grid=(n // tm, c // tn),
        in_specs=[
            pl.BlockSpec((tm, d), lambda i, j: (i, 0)),
            pl.BlockSpec((1, d), lambda i, j: (0, 0)),
            pl.BlockSpec((None, d, tn), lambda i, j: (l, 0, j)),
        ],
        out_specs=pl.BlockSpec((tm, tn), lambda i, j: (i, j)),
        out_shape=jax.ShapeDtypeStruct((n, c), BF16),
        scratch_shapes=[pltpu.VMEM((tm, d), BF16)],
        compiler_params=_params("arbitrary", "arbitrary"),
        name="mixer_in",
    )(x, g, w)


def _mixer_mid_kernel(x_ref, z_ref, zh_ref, pm_ref, ps_ref, wpo_ref, cw_ref, wco_ref, wmo_ref, o_ref,
                      *, tiles_per_seq, tm, d):
    pw = d // 2
    gw = pw // len(POOL_WINDOWS)
    t = pl.program_id(0) % tiles_per_seq
    keep = jnp.where(t == 0, 0.0, 1.0).astype(F32)
    pos = lax.broadcasted_iota(jnp.int32, (tm, gw), 0) + (t * tm + 1)

    parts = []
    for gi, w in enumerate(POOL_WINDOWS):
        cols = slice(gi * gw, (gi + 1) * gw)
        u = z_ref[:, cols].astype(F32)
        ext = jnp.concatenate([zh_ref[:, cols].astype(F32) * keep, u], axis=0)
        s = 1
        while s < w:
            ext = ext + pltpu.roll(ext, s, axis=0)
            s *= 2
        cnt = jnp.minimum(pos, w).astype(F32)
        p = (ext[HALO:, :] / cnt - u).astype(BF16)
        m = _dot(p, pm_ref[gi]) * ps_ref[:, cols]
        parts.append(m.astype(BF16))
    br_a = _dot(jnp.concatenate(parts, axis=1), wpo_ref[...])

    uc = z_ref[:, pw:2 * pw].astype(F32)
    bg = z_ref[:, 2 * pw:3 * pw].astype(F32)
    cg = z_ref[:, 3 * pw:4 * pw].astype(F32)
    v = cg * uc
    vh = zh_ref[:, 3 * pw:4 * pw].astype(F32) * zh_ref[:, pw:2 * pw].astype(F32) * keep
    vext = jnp.concatenate([vh, v], axis=0)
    y = cw_ref[CONV_K - 1:CONV_K, :] * v
    for j in range(CONV_K - 1):
        y = y + cw_ref[j:j + 1, :] * pltpu.roll(vext, CONV_K - 1 - j, axis=0)[HALO:, :]
    br_b = _dot((bg * y).astype(BF16), wco_ref[...])

    ga = z_ref[:, 2 * d:3 * d].astype(F32)
    gb = z_ref[:, 3 * d:4 * d].astype(F32)
    merged = jax.nn.sigmoid(ga) * br_a + jax.nn.sigmoid(gb) * br_b
    o_ref[...] = x_ref[...] + _dot(merged.astype(BF16), wmo_ref[...])


def _mixer_mid(x, z, pool_mix, pool_scale, w_pool_out, conv_w, w_conv_out, w_mix_out, l, seq, tm):
    n, d = x.shape
    c = z.shape[1]
    hb = tm // HALO
    kern = functools.partial(_mixer_mid_kernel, tiles_per_seq=seq // tm, tm=tm, d=d)
    return pl.pallas_call(
        kern,
        grid=(n // tm,),
        in_specs=[
            pl.BlockSpec((tm, d), lambda i: (i, 0)),
            pl.BlockSpec((tm, c), lambda i: (i, 0)),
            pl.BlockSpec((HALO, c // 2), lambda i: (jnp.maximum(i * hb - 1, 0), 0)),
            _resident(pool_mix, l),
            _resident(pool_scale, l),
            _resident(w_pool_out, l),
            _resident(conv_w, l),
            _resident(w_conv_out, l),
            _resident(w_mix_out, l),
        ],
        out_specs=pl.BlockSpec((tm, d), lambda i: (i, 0)),
        out_shape=jax.ShapeDtypeStruct((n, d), F32),
        compiler_params=_params("arbitrary"),
        name="mixer_mid",
    )(x, z, z, pool_mix, pool_scale, w_pool_out, conv_w, w_conv_out, w_mix_out)


def _kv_kernel(m_ref, g_ref, wk_ref, wv_ref, o_ref, *, nb):
    mn = _rms(m_ref[...], g_ref[...]).astype(BF16)
    j = pl.program_id(0)

    @pl.when(j < nb)
    def _():
        o_ref[...] = _dot(mn, wk_ref[...]).astype(o_ref.dtype)

    @pl.when(j >= nb)
    def _():
        o_ref[...] = _dot(mn, wv_ref[...]).astype(o_ref.dtype)


def _kv_proj(mem, g, w_k, w_v, l, tn):
    r, d = mem.shape
    nb = w_k.shape[2] // tn
    return pl.pallas_call(
        functools.partial(_kv_kernel, nb=nb),
        grid=(2 * nb,),
        in_specs=[
            pl.BlockSpec((r, d), lambda j: (0, 0)),
            pl.BlockSpec((1, d), lambda j: (0, 0)),
            pl.BlockSpec((None, d, tn), lambda j: (l, 0, jnp.minimum(j, nb - 1))),
            pl.BlockSpec((None, d, tn), lambda j: (l, 0, jnp.maximum(j - nb, 0))),
        ],
        out_specs=pl.BlockSpec((r, tn), lambda j: (0, j)),
        out_shape=jax.ShapeDtypeStruct((r, 2 * nb * tn), BF16),
        compiler_params=_params("arbitrary"),
        name="kv_proj",
    )(mem, g, w_k, w_v)


def _attend_rows(x, g, wq_ref, kv_ref, wo_ref, d):
    hd = d // XA_HEADS
    scale = 1.0 / math.sqrt(hd)
    q = _dot(_rms(x, g).astype(BF16), wq_ref[...]).astype(BF16)
    heads = []
    for h in range(XA_HEADS):
        qh = q[:, h * hd:(h + 1) * hd]
        kh = kv_ref[:, h * hd:(h + 1) * hd]
        vh = kv_ref[:, d + h * hd:d + (h + 1) * hd]
        s = lax.dot_general(qh, kh, (((1,), (1,)), ((), ())), preferred_element_type=F32) * scale
        e = jnp.exp(s - jnp.max(s, axis=-1, keepdims=True))
        p = (e / jnp.sum(e, axis=-1, keepdims=True)).astype(BF16)
        heads.append(_dot(p, vh).astype(BF16))
    return x + _dot(jnp.concatenate(heads, axis=1), wo_ref[...])


def _xattn_kernel(x_ref, g_ref, wq_ref, kv_ref, wo_ref, o_ref, *, d):
    o_ref[...] = _attend_rows(x_ref[...], g_ref[...], wq_ref, kv_ref, wo_ref, d)


def _xattn(x, g, w_q, kv, w_o, l, seq, n_mem, tm):
    n, d = x.shape
    tiles_per_seq = seq // tm
    return pl.pallas_call(
        functools.partial(_xattn_kernel, d=d),
        grid=(n // tm,),
        in_specs=[
            pl.BlockSpec((tm, d), lambda i: (i, 0)),
            pl.BlockSpec((1, d), lambda i: (0, 0)),
            _resident(w_q, l),
            pl.BlockSpec((n_mem, 2 * d), lambda i: (i // tiles_per_seq, 0)),
            _resident(w_o, l),
        ],
        out_specs=pl.BlockSpec((tm, d), lambda i: (i, 0)),
        out_shape=jax.ShapeDtypeStruct((n, d), F32),
        compiler_params=_params("arbitrary"),
        name="xattn",
    )(x, g, w_q, kv, w_o)


def _pack_pairs(v):
    c = v.shape[1] // 2
    bits = lax.bitcast_convert_type(v.astype(BF16).astype(F32), jnp.uint32)
    return (bits[:, :c] >> 16) | bits[:, c:]


def _unpack_pairs(w):
    lo = lax.bitcast_convert_type(w << 16, F32)
    hi = lax.bitcast_convert_type(w & jnp.uint32(0xFFFF0000), F32)
    return lo, hi


ROW_E1, ROW_E2, ROW_R1_HI, ROW_R1_LO, ROW_R2_HI, ROW_R2_LO = range(6)
ROUTE_ROWS = 8
RANK_RADIX = 256


def _route_rows(x, g, wr, br, carry):
    tm = x.shape[0]
    h = _rms(x, g)
    h_hi = h.astype(BF16)
    h_lo = (h - h_hi.astype(F32)).astype(BF16)
    logits = _dot(jnp.concatenate([h_hi, h_lo, h_hi], axis=1), wr) + br
    lane = lax.broadcasted_iota(jnp.int32, logits.shape, 1)
    neg = jnp.float32(-jnp.inf)
    big = jnp.int32(ROUTE_LANES)

    is_g = (lane >= N_EXPERTS) & (lane < N_EXPERTS + N_GROUPS)
    lg = jnp.where(is_g, logits, neg)
    gmax = jnp.max(lg, axis=-1, keepdims=True)
    g_sel = jnp.min(jnp.where(lg == gmax, lane - N_EXPERTS, big), axis=-1, keepdims=True)
    p_group = 1.0 / jnp.sum(jnp.exp(lg - gmax), axis=-1, keepdims=True)

    in_group = (lane >= g_sel * EXPERTS_PER_GROUP) & (lane < (g_sel + 1) * EXPERTS_PER_GROUP)
    le = jnp.where(in_group, logits, neg)
    v1 = jnp.max(le, axis=-1, keepdims=True)
    i1 = jnp.min(jnp.where(le == v1, lane, big), axis=-1, keepdims=True)
    le2 = jnp.where(lane == i1, neg, le)
    v2 = jnp.max(le2, axis=-1, keepdims=True)
    i2 = jnp.min(jnp.where(le2 == v2, lane, big), axis=-1, keepdims=True)
    t = jnp.exp(v2 - v1)
    g1 = p_group / (1.0 + t)
    g2 = p_group * t / (1.0 + t)

    oh1 = lane == i1
    oh2 = lane == i2
    both = jnp.where(oh1, 1.0, 0.0) + jnp.where(oh2, 1.0, 0.0)
    rr = lax.broadcasted_iota(jnp.int32, (tm, tm), 0)
    cc = lax.broadcasted_iota(jnp.int32, (tm, tm), 1)
    lower = jnp.where(cc < rr, 1.0, 0.0).astype(BF16)
    before = _dot(lower, both.astype(BF16)) + carry
    rank1 = jnp.sum(jnp.where(oh1, before, 0.0), axis=-1, keepdims=True)
    rank2 = jnp.sum(jnp.where(oh2, before, 0.0), axis=-1, keepdims=True)

    gates = jnp.where(lane == 2, g1, 0.0)
    gates = jnp.where(lane == 3, g2, gates)

    r1_hi = jnp.floor(rank1 * (1.0 / RANK_RADIX))
    r2_hi = jnp.floor(rank2 * (1.0 / RANK_RADIX))
    vals = jnp.where(lane == ROW_E1, i1.astype(F32), 0.0)
    vals = jnp.where(lane == ROW_E2, i2.astype(F32), vals)
    vals = jnp.where(lane == ROW_R1_HI, r1_hi, vals)
    vals = jnp.where(lane == ROW_R1_LO, rank1 - r1_hi * RANK_RADIX, vals)
    vals = jnp.where(lane == ROW_R2_HI, r2_hi, vals)
    vals = jnp.where(lane == ROW_R2_LO, rank2 - r2_hi * RANK_RADIX, vals)
    pick = jnp.where(lax.broadcasted_iota(jnp.int32, (ROUTE_ROWS, ROUTE_LANES), 0)
                     == lax.broadcasted_iota(jnp.int32, (ROUTE_ROWS, ROUTE_LANES), 1), 1.0, 0.0).astype(BF16)
    rows = lax.dot_general(pick, vals.astype(BF16), (((1,), (1,)), ((), ())), preferred_element_type=F32)
    return _pack_pairs(h), gates, rows, carry + jnp.sum(both, axis=0, keepdims=True)


def _route_kernel(x_ref, g_ref, wr_ref, br_ref, hp_ref, r_ref, rows_ref, cnt_ref, carry_scr):
    @pl.when(pl.program_id(0) == 0)
    def _():
        carry_scr[...] = jnp.zeros_like(carry_scr)

    hp, gates, row_form, carry = _route_rows(x_ref[...], g_ref[...], wr_ref[...], br_ref[...], carry_scr[...])
    hp_ref[...] = hp
    r_ref[...] = gates
    rows_ref[...] = row_form
    carry_scr[...] = carry
    cnt_ref[...] = carry


def _route(x, g, w_r, b_r, tm):
    n, d = x.shape
    return pl.pallas_call(
        _route_kernel,
        grid=(n // tm,),
        in_specs=[
            pl.BlockSpec((tm, d), lambda i: (i, 0)),
            pl.BlockSpec((1, d), lambda i: (0, 0)),
            pl.BlockSpec(w_r.shape, lambda i: (0, 0)),
            pl.BlockSpec((1, ROUTE_LANES), lambda i: (0, 0)),
        ],
        out_specs=[
            pl.BlockSpec((tm, d // 2), lambda i: (i, 0)),
            pl.BlockSpec((tm, ROUTE_LANES), lambda i: (i, 0)),
            pl.BlockSpec((ROUTE_ROWS, tm), lambda i: (0, i)),
            pl.BlockSpec((1, ROUTE_LANES), lambda i: (0, 0)),
        ],
        out_shape=[
            jax.ShapeDtypeStruct((n, d // 2), jnp.uint32),
            jax.ShapeDtypeStruct((n, ROUTE_LANES), F32),
            jax.ShapeDtypeStruct((ROUTE_ROWS, n), F32),
            jax.ShapeDtypeStruct((1, ROUTE_LANES), F32),
        ],
        scratch_shapes=[pltpu.VMEM((1, ROUTE_LANES), F32)],
        compiler_params=_params("arbitrary"),
        name="route",
    )(x, g, w_r, b_r)


def _moe_plan(rows, cnt, tm):
    n = rows.shape[1]
    ri = rows.astype(jnp.int32)
    experts = jnp.arange(N_EXPERTS, dtype=jnp.int32)
    counts = cnt[0, :N_EXPERTS].astype(jnp.int32)
    padded = (counts + tm - 1) // tm * tm
    pends = jnp.sum(jnp.where(experts[None, :] <= experts[:, None], padded[None, :], 0), axis=1)
    pstarts = pends - padded

    def dest(e, hi, lo):
        start = jnp.sum(jnp.where(e[:, None] == experts[None, :], pstarts[None, :], 0), axis=1)
        return start + hi * RANK_RADIX + lo

    dest_flat = jnp.concatenate([dest(ri[ROW_E1], ri[ROW_R1_HI], ri[ROW_R1_LO]),
                                 dest(ri[ROW_E2], ri[ROW_R2_HI], ri[ROW_R2_LO])])
    n_tiles = (2 * n + N_EXPERTS * tm) // tm
    n_used = (pends[-1] // tm).astype(jnp.int32)
    tile_start = jnp.arange(n_tiles, dtype=jnp.int32) * tm
    te = jnp.sum((pends[None, :] <= tile_start[:, None]).astype(jnp.int32), axis=1)
    last_used = jnp.max(jnp.where(counts > 0, experts, 0))
    te = jnp.minimum(te, last_used)
    return dest_flat, counts, pends, te, n_used.reshape(1)


def _dispatch_kernel(dest_ref, cnt_ref, pend_ref, h_ref, xp_hbm, zero_scr, sem, zsem, *, tm, tile, n):
    i = pl.program_id(0)

    @pl.when(i == 0)
    def _():
        zero_scr[...] = jnp.zeros_like(zero_scr)

        def zero_tile(t, carry):
            cp = pltpu.make_async_copy(zero_scr, xp_hbm.at[pl.ds(pl.multiple_of(t * tile, tile), tile)], zsem)
            cp.start()
            cp.wait()
            return carry

        lax.fori_loop(pend_ref[N_EXPERTS - 1] // tile, xp_hbm.shape[0] // tile, zero_tile, 0)
        for e in range(N_EXPERTS):
            @pl.when(cnt_ref[e] > 0)
            def _():
                start = pl.multiple_of(pend_ref[e] - tile, tile)
                pltpu.make_async_copy(zero_scr, xp_hbm.at[pl.ds(start, tile)], zsem).start()
        for e in range(N_EXPERTS):
            @pl.when(cnt_ref[e] > 0)
            def _():
                start = pl.multiple_of(pend_ref[e] - tile, tile)
                pltpu.make_async_copy(zero_scr, xp_hbm.at[pl.ds(start, tile)], zsem).wait()

    base = i * tm

    def row_copy(r, k):
        return pltpu.make_async_copy(h_ref.at[pl.ds(r, 1)], xp_hbm.at[pl.ds(dest_ref[k * n + base + r], 1)], sem)

    copies = [row_copy(r, k) for r in range(tm) for k in range(2)]
    for ci, cp in enumerate(copies):
        cp.start(priority=ci % ROW_DMA_QUEUES)
    for cp in copies:
        cp.wait()


def _dispatch(dest_flat, counts, pends, hp, tm, tile):
    n, c = hp.shape
    rows = 2 * n + N_EXPERTS * tile
    grid_spec = pltpu.PrefetchScalarGridSpec(
        num_scalar_prefetch=3,
        grid=(n // tm,),
        in_specs=[pl.BlockSpec((tm, c), lambda i, *_: (i, 0))],
        out_specs=pl.BlockSpec(memory_space=pl.ANY),
        scratch_shapes=[pltpu.VMEM((tile, c), jnp.uint32), pltpu.SemaphoreType.DMA, pltpu.SemaphoreType.DMA],
    )
    return pl.pallas_call(
        functools.partial(_dispatch_kernel, tm=tm, tile=tile, n=n),
        grid_spec=grid_spec,
        out_shape=jax.ShapeDtypeStruct((rows, c), jnp.uint32),
        compiler_params=_params("arbitrary"),
        name="dispatch",
    )(dest_flat, counts, pends, hp)


N_WGROUPS = 8
N_WBUF = 2


def _gmm_plan(te, n_used, counts, pends, tm):
    n_tiles = te.shape[0]
    idx = jnp.arange(n_tiles, dtype=jnp.int32)
    experts = jnp.arange(N_EXPERTS, dtype=jnp.int32)
    by_tile = lambda table: jnp.sum(jnp.where(te[:, None] == experts[None, :], table[None, :], 0), axis=1)
    run_len = by_tile((counts + tm - 1) // tm)
    pos = idx - by_tile(pends // tm - (counts + tm - 1) // tm)
    later = (experts[None, :] > experts[:, None]) & (counts[None, :] > 0)
    nxt_e = jnp.min(jnp.where(later, experts[None, :], N_EXPERTS), axis=1)
    nxt = by_tile(jnp.where(nxt_e < N_EXPERTS, nxt_e, -1))
    used_before = jnp.sum(jnp.where((experts[None, :] < experts[:, None]) & (counts[None, :] > 0), 1, 0), axis=1)
    slot = by_tile(used_before % 2)
    safe_len = jnp.maximum(run_len, 1)
    g0 = pos * N_WGROUPS // safe_len
    kk = (pos + 1) * N_WGROUPS // safe_len - g0
    first = (pos == 0).astype(jnp.int32)
    fix = lambda v: jnp.where(idx < n_used, v, 0).astype(jnp.int32)
    return fix(nxt) - (idx >= n_used).astype(jnp.int32), fix(slot), fix(first), fix(g0), fix(kk)


def _gmm_kernel(te_ref, nu_ref, nxt_ref, slot_ref, first_ref, g0_ref, kk_ref, x_ref, wg_hbm, wu_hbm, wd_hbm, y_ref,
                wg_s, wu_s, wd_s, stg_g, stg_u, stg_d, sem, *, l):
    i = pl.program_id(0)
    used = i < nu_ref[0]
    ra = wg_s.shape[1] // N_WGROUPS
    rd = wd_s.shape[1] // N_WGROUPS

    def group_copies(e, g):
        b = g % N_WBUF
        a0 = pl.multiple_of(g * ra, ra)
        d0 = pl.multiple_of(g * rd, rd)
        return (pltpu.make_async_copy(wg_hbm.at[l, e, pl.ds(a0, ra)], stg_g.at[b], sem.at[0, b]),
                pltpu.make_async_copy(wu_hbm.at[l, e, pl.ds(a0, ra)], stg_u.at[b], sem.at[1, b]),
                pltpu.make_async_copy(wd_hbm.at[l, e, pl.ds(d0, rd)], stg_d.at[b], sem.at[2, b]))

    def start_group(e, g):
        for cp in group_copies(e, g):
            cp.start()

    def begin_load(e):
        for g in range(N_WBUF):
            start_group(e, g)

    def cast_groups(e, slot, first_g, n_g):
        def body(it, carry):
            g = first_g + it
            b = g % N_WBUF
            for cp in group_copies(e, g):
                cp.wait()
            a0 = pl.multiple_of(g * ra, ra)
            d0 = pl.multiple_of(g * rd, rd)
            wg_s[slot, pl.ds(a0, ra), :] = stg_g[b].astype(BF16)
            wu_s[slot, pl.ds(a0, ra), :] = stg_u[b].astype(BF16)
            wd_s[slot, pl.ds(d0, rd), :] = stg_d[b].astype(BF16)

            @pl.when(g + N_WBUF < N_WGROUPS)
            def _():
                start_group(e, g + N_WBUF)
            return carry
        lax.fori_loop(0, n_g, body, 0)

    @pl.when(used)
    def _():
        slot = slot_ref[i]

        @pl.when(i == 0)
        def _():
            begin_load(te_ref[0])
            cast_groups(te_ref[0], slot, 0, N_WGROUPS)

        nxt = nxt_ref[i]

        @pl.when(nxt >= 0)
        def _():
            @pl.when(first_ref[i] == 1)
            def _():
                begin_load(nxt)
            cast_groups(nxt, 1 - slot, g0_ref[i], kk_ref[i])

        lo, hi = _unpack_pairs(x_ref[...])
        x = jnp.concatenate([lo.astype(BF16), hi.astype(BF16)], axis=1)
        a = _dot(x, wg_s[slot])
        b = _dot(x, wu_s[slot])
        hid = (a * jax.nn.sigmoid(a) * b).astype(BF16)
        y_ref[...] = _pack_pairs(_dot(hid, wd_s[slot]))

    @pl.when(jnp.logical_not(used))
    def _():
        y_ref[...] = jnp.zeros_like(y_ref)


def _gmm(tile_expert, n_used, plan, x_pad, w_gate, w_up, w_down, l, tm):
    rows, c = x_pad.shape
    _, _, d, de = w_gate.shape

    def row_map(i, te, nu, *_):
        return (jnp.minimum(i, jnp.maximum(nu[0] - 1, 0)), 0)

    grid_spec = pltpu.PrefetchScalarGridSpec(
        num_scalar_prefetch=2 + len(plan),
        grid=(rows // tm,),
        in_specs=[
            pl.BlockSpec((tm, c), row_map),
            pl.BlockSpec(memory_space=pl.ANY),
            pl.BlockSpec(memory_space=pl.ANY),
            pl.BlockSpec(memory_space=pl.ANY),
        ],
        out_specs=pl.BlockSpec((tm, c), lambda i, *_: (i, 0)),
        scratch_shapes=[
            pltpu.VMEM((2, d, de), BF16),
            pltpu.VMEM((2, d, de), BF16),
            pltpu.VMEM((2, de, d), BF16),
            pltpu.VMEM((N_WBUF, d // N_WGROUPS, de), F32),
            pltpu.VMEM((N_WBUF, d // N_WGROUPS, de), F32),
            pltpu.VMEM((N_WBUF, de // N_WGROUPS, d), F32),
            pltpu.SemaphoreType.DMA((3, N_WBUF)),
        ],
    )
    return pl.pallas_call(
        functools.partial(_gmm_kernel, l=l),
        grid_spec=grid_spec,
        out_shape=jax.ShapeDtypeStruct((rows, c), jnp.uint32),
        compiler_params=_params("arbitrary"),
        name="gmm",
    )(tile_expert, n_used, *plan, x_pad, w_gate, w_up, w_down)


def _combine_kernel(dest_ref, x_ref, r_ref, g_ref, yp_hbm, o_ref, ybuf, sem, *, tm, n, final_norm):
    j = pl.program_id(0)
    n_tiles = pl.num_programs(0) - 1
    c = ybuf.shape[-1]

    def row_copy(src_row, slot, r, k):
        return pltpu.make_async_copy(yp_hbm.at[pl.ds(src_row, 1)], ybuf.at[slot, k, pl.ds(r, 1)], sem.at[slot])

    def start_gather(slot):
        for r in range(tm):
            for k in range(2):
                row_copy(dest_ref[k * n + j * tm + r], slot, r, k).start(priority=(2 * r + k) % ROW_DMA_QUEUES)

    def finish_tile(slot):
        for r in range(tm):
            for k in range(2):
                row_copy(0, slot, r, k).wait()
        g1 = r_ref[:, 2:3]
        g2 = r_ref[:, 3:4]
        lo1, hi1 = _unpack_pairs(ybuf[slot, 0])
        lo2, hi2 = _unpack_pairs(ybuf[slot, 1])
        x = x_ref[...]
        out = jnp.concatenate([x[:, :c] + g1 * lo1 + g2 * lo2, x[:, c:] + g1 * hi1 + g2 * hi2], axis=1)
        o_ref[...] = _rms(out, g_ref[...]) if final_norm else out

    for slot in range(2):
        pl.when((j < n_tiles) & (j % 2 == slot))(functools.partial(start_gather, slot))
    for slot in range(2):
        pl.when((j > 0) & ((j - 1) % 2 == slot))(functools.partial(finish_tile, slot))


def _combine(dest_flat, x, route, g, y_pad, tm, final_norm):
    n, d = x.shape
    c = y_pad.shape[1]
    grid_spec = pltpu.PrefetchScalarGridSpec(
        num_scalar_prefetch=1,
        grid=(n // tm + 1,),
        in_specs=[
            pl.BlockSpec((tm, d), lambda j, *_: (jnp.maximum(j - 1, 0), 0)),
            pl.BlockSpec((tm, ROUTE_LANES), lambda j, *_: (jnp.maximum(j - 1, 0), 0)),
            pl.BlockSpec((1, d), lambda j, *_: (0, 0)),
            pl.BlockSpec(memory_space=pl.ANY),
        ],
        out_specs=pl.BlockSpec((tm, d), lambda j, *_: (jnp.maximum(j - 1, 0), 0)),
        scratch_shapes=[pltpu.VMEM((2, 2, tm, c), jnp.uint32), pltpu.SemaphoreType.DMA((2,))],
    )
    return pl.pallas_call(
        functools.partial(_combine_kernel, tm=tm, n=n, final_norm=final_norm),
        grid_spec=grid_spec,
        out_shape=jax.ShapeDtypeStruct((n, d), F32),
        compiler_params=_params("arbitrary"),
        name="combine",
    )(dest_flat, x, route, g, y_pad)


def _tile(n, want):
    return min(n, want)


def kernel(x, mem, w_in, pool_mix, pool_scale, w_pool_out, conv_w, w_conv_out, w_mix_out, g_mix, g_xattn, g_mem, w_q, w_k, w_v, w_o, g_ffn, w_route_group, b_route_group, w_route_expert, b_route_expert, w_gate, w_up, w_down, g_final):
    bsz, seq, d = x.shape
    n_mem = mem.shape[1]
    depth = w_in.shape[0]
    n = bsz * seq
    xf = x.reshape(n, d)
    memf = mem.reshape(bsz * n_mem, d)

    tm_in = _tile(seq, 1024)
    tm_mid = _tile(seq, 256)
    tm_att = _tile(seq, 512)
    tm_route = _tile(seq, 512)
    tm_disp = _tile(seq, 512)
    tm_gmm = 256
    tm_comb = _tile(seq, 256)

    bf = lambda w: w.astype(BF16)
    w_in, pool_mix, w_pool_out, w_conv_out, w_mix_out = map(bf, (w_in, pool_mix, w_pool_out, w_conv_out, w_mix_out))
    w_q, w_k, w_v, w_o = map(bf, (w_q, w_k, w_v, w_o))
    pool_scale = pool_scale.reshape(depth, 1, -1)

    pad_lanes = ROUTE_LANES - N_EXPERTS - N_GROUPS
    for l in range(depth):
        row = lambda v: v[l].reshape(1, -1)
        z = _mixer_in(xf, row(g_mix), w_in, l, tm_in, _tile(w_in.shape[2], 2048))
        xf = _mixer_mid(xf, z, pool_mix, pool_scale, w_pool_out, conv_w, w_conv_out, w_mix_out, l, seq, tm_mid)

        kv = _kv_proj(memf, row(g_mem), w_k, w_v, l, _tile(d, 1024))
        w_r = jnp.concatenate([w_route_expert[l], w_route_group[l], jnp.zeros((d, pad_lanes), F32)], axis=1)
        w_r_hi = w_r.astype(BF16)
        w_r_lo = (w_r - w_r_hi.astype(F32)).astype(BF16)
        w_r3 = jnp.concatenate([w_r_hi, w_r_hi, w_r_lo], axis=0)
        b_r = jnp.concatenate([b_route_expert[l], b_route_group[l], jnp.zeros((pad_lanes,), F32)]).reshape(1, -1)
        xf = _xattn(xf, row(g_xattn), w_q, kv, w_o, l, seq, n_mem, tm_att)
        hp, route, rows, cnt = _route(xf, row(g_ffn), w_r3, b_r, tm_route)
        dest_flat, counts, pends, te, n_used = _moe_plan(rows, cnt, tm_gmm)
        x_pad = _dispatch(dest_flat, counts, pends, hp, tm_disp, tm_gmm)
        plan = _gmm_plan(te, n_used, counts, pends, tm_gmm)
        y_pad = _gmm(te, n_used, plan, x_pad, w_gate, w_up, w_down, l, tm_gmm)
        xf = _combine(dest_flat, xf, route, g_final.reshape(1, -1), y_pad, tm_comb, final_norm=(l == depth - 1))
    return xf.reshape(bsz, seq, d)
```

```python
import functools
import math

import jax
import jax.numpy as jnp
from jax import lax
from jax.experimental import pallas as pl
from jax.experimental.pallas import tpu as pltpu

EPS = 1e-6
POOL_WINDOWS = (2, 4, 8, 16)
CONV_K = 3
XA_HEADS = 4
N_GROUPS = 4
EXPERTS_PER_GROUP = 8
N_EXPERTS = N_GROUPS * EXPERTS_PER_GROUP
HALO = 16
ROW_DMA_QUEUES = 2
ROUTE_LANES = 128
VMEM_LIMIT = 56 * 1024 * 1024

F32 = jnp.float32
BF16 = jnp.bfloat16


def _params(*sem):
    return pltpu.CompilerParams(dimension_semantics=sem, vmem_limit_bytes=VMEM_LIMIT)


def _resident(stack, l):
    nd = stack.ndim - 1
    return pl.BlockSpec((None,) + stack.shape[1:], lambda *_: (l,) + (0,) * nd, pipeline_mode=pl.Buffered(1))


def _rms(xf, g):
    ms = jnp.mean(xf * xf, axis=-1, keepdims=True)
    return xf * lax.rsqrt(ms + EPS) * g


def _dot(a, b):
    return jnp.dot(a, b, preferred_element_type=F32)


def _mixer_in_kernel(x_ref, g_ref, w_ref, z_ref, h_scr):
    @pl.when(pl.program_id(1) == 0)
    def _():
        h_scr[...] = _rms(x_ref[...], g_ref[...]).astype(BF16)

    z_ref[...] = _dot(h_scr[...], w_ref[...]).astype(z_ref.dtype)


def _mixer_in(x, g, w, l, tm, tn):
    n, d = x.shape
    c = w.shape[2]
    return pl.pallas_call(
        _mixer_in_kernel,
        grid=(n // tm, c // tn),
        in_specs=[
            pl.BlockSpec((tm, d), lambda i, j: (i, 0)),
            pl.BlockSpec((1, d), lambda i, j: (0, 0)),
            pl.BlockSpec((None, d, tn), lambda i, j: (l, 0, j)),
        ],
        out_specs=pl.BlockSpec((tm, tn), lambda i, j: (i, j)),
        out_shape=jax.ShapeDtypeStruct((n, c), BF16),
        scratch_shapes=[pltpu.VMEM((tm, d), BF16)],
        compiler_params=_params("arbitrary", "arbitrary"),
        name="mixer_in",
    )(x, g, w)


def _mixer_mid_kernel(x_ref, z_ref, zh_ref, pm_ref, ps_ref, wpo_ref, cw_ref, wco_ref, wmo_ref, o_ref,
                      *, tiles_per_seq, tm, d):
    pw = d // 2
    gw = pw // len(POOL_WINDOWS)
    t = pl.program_id(0) % tiles_per_seq
    keep = jnp.where(t == 0, 0.0, 1.0).astype(F32)
    pos = lax.broadcasted_iota(jnp.int32, (tm, gw), 0) + (t * tm + 1)

    uc = z_ref[:, pw:2 * pw].astype(F32)
    bg = z_ref[:, 2 * pw:3 * pw].astype(F32)
    cg = z_ref[:, 3 * pw:4 * pw].astype(F32)
    v = cg * uc
    vh = zh_ref[:, 3 * pw:4 * pw].astype(F32) * zh_ref[:, pw:2 * pw].astype(F32) * keep
    vext = jnp.concatenate([vh, v], axis=0)
    y = cw_ref[CONV_K - 1:CONV_K, :] * v
    for j in range(CONV_K - 1):
        y = y + cw_ref[j:j + 1, :] * pltpu.roll(vext, CONV_K - 1 - j, axis=0)[HALO:, :]
    br_b = _dot((bg * y).astype(BF16), wco_ref[...])
    gated_b = jax.nn.sigmoid(z_ref[:, 3 * d:4 * d].astype(F32)) * br_b

    parts = []
    for gi, w in enumerate(POOL_WINDOWS):
        cols = slice(gi * gw, (gi + 1) * gw)
        u = z_ref[:, cols].astype(F32)
        ext = jnp.concatenate([zh_ref[:, cols].astype(F32) * keep, u], axis=0)
        s = 1
        while s < w:
            ext = ext + pltpu.roll(ext, s, axis=0)
            s *= 2
        cnt = jnp.minimum(pos, w).astype(F32)
        p = (ext[HALO:, :] / cnt - u).astype(BF16)
        m = _dot(p, pm_ref[gi]) * ps_ref[:, cols]
        parts.append(m.astype(BF16))
    br_a = _dot(jnp.concatenate(parts, axis=1), wpo_ref[...])

    merged = jax.nn.sigmoid(z_ref[:, 2 * d:3 * d].astype(F32)) * br_a + gated_b
    o_ref[...] = x_ref[...] + _dot(merged.astype(BF16), wmo_ref[...])


def _mixer_mid(x, z, pool_mix, pool_scale, w_pool_out, conv_w, w_conv_out, w_mix_out, l, seq, tm):
    n, d = x.shape
    c = z.shape[1]
    hb = tm // HALO
    kern = functools.partial(_mixer_mid_kernel, tiles_per_seq=seq // tm, tm=tm, d=d)
    return pl.pallas_call(
        kern,
        grid=(n // tm,),
        in_specs=[
            pl.BlockSpec((tm, d), lambda i: (i, 0)),
            pl.BlockSpec((tm, c), lambda i: (i, 0)),
            pl.BlockSpec((HALO, c // 2), lambda i: (jnp.maximum(i * hb - 1, 0), 0)),
            _resident(pool_mix, l),
            _resident(pool_scale, l),
            _resident(w_pool_out, l),
            _resident(conv_w, l),
            _resident(w_conv_out, l),
            _resident(w_mix_out, l),
        ],
        out_specs=pl.BlockSpec((tm, d), lambda i: (i, 0)),
        out_shape=jax.ShapeDtypeStruct((n, d), F32),
        compiler_params=_params("arbitrary"),
        name="mixer_mid",
    )(x, z, z, pool_mix, pool_scale, w_pool_out, conv_w, w_conv_out, w_mix_out)


def _kv_kernel(m_ref, g_ref, wk_ref, wv_ref, o_ref, *, nb):
    mn = _rms(m_ref[...], g_ref[...]).astype(BF16)
    j = pl.program_id(0)

    @pl.when(j < nb)
    def _():
        o_ref[...] = _dot(mn, wk_ref[...]).astype(o_ref.dtype)

    @pl.when(j >= nb)
    def _():
        o_ref[...] = _dot(mn, wv_ref[...]).astype(o_ref.dtype)


def _kv_proj(mem, g, w_k, w_v, l, tn):
    r, d = mem.shape
    nb = w_k.shape[2] // tn
    return pl.pallas_call(
        functools.partial(_kv_kernel, nb=nb),
        grid=(2 * nb,),
        in_specs=[
            pl.BlockSpec((r, d), lambda j: (0, 0)),
            pl.BlockSpec((1, d), lambda j: (0, 0)),
            pl.BlockSpec((None, d, tn), lambda j: (l, 0, jnp.minimum(j, nb - 1))),
            pl.BlockSpec((None, d, tn), lambda j: (l, 0, jnp.maximum(j - nb, 0))),
        ],
        out_specs=pl.BlockSpec((r, tn), lambda j: (0, j)),
        out_shape=jax.ShapeDtypeStruct((r, 2 * nb * tn), BF16),
        compiler_params=_params("arbitrary"),
        name="kv_proj",
    )(mem, g, w_k, w_v)


def _attend_rows(x, g, wq_ref, kv_ref, wo_ref, d):
    hd = d // XA_HEADS
    scale = 1.0 / math.sqrt(hd)
    q = _dot(_rms(x, g).astype(BF16), wq_ref[...]).astype(BF16)
    heads = []
    for h in range(XA_HEADS):
        qh = q[:, h * hd:(h + 1) * hd]
        kh = kv_ref[:, h * hd:(h + 1) * hd]
        vh = kv_ref[:, d + h * hd:d + (h + 1) * hd]
        s = lax.dot_general(qh, kh, (((1,), (1,)), ((), ())), preferred_element_type=F32) * scale
        e = jnp.exp(s - jnp.max(s, axis=-1, keepdims=True))
        p = (e / jnp.sum(e, axis=-1, keepdims=True)).astype(BF16)
        heads.append(_dot(p, vh).astype(BF16))
    return x + _dot(jnp.concatenate(heads, axis=1), wo_ref[...])


def _xattn_kernel(x_ref, g_ref, wq_ref, kv_ref, wo_ref, o_ref, *, d):
    o_ref[...] = _attend_rows(x_ref[...], g_ref[...], wq_ref, kv_ref, wo_ref, d)


def _xattn(x, g, w_q, kv, w_o, l, seq, n_mem, tm):
    n, d = x.shape
    tiles_per_seq = seq // tm
    return pl.pallas_call(
        functools.partial(_xattn_kernel, d=d),
        grid=(n // tm,),
        in_specs=[
            pl.BlockSpec((tm, d), lambda i: (i, 0)),
            pl.BlockSpec((1, d), lambda i: (0, 0)),
            _resident(w_q, l),
            pl.BlockSpec((n_mem, 2 * d), lambda i: (i // tiles_per_seq, 0)),
            _resident(w_o, l),
        ],
        out_specs=pl.BlockSpec((tm, d), lambda i: (i, 0)),
        out_shape=jax.ShapeDtypeStruct((n, d), F32),
        compiler_params=_params("arbitrary"),
        name="xattn",
    )(x, g, w_q, kv, w_o)


def _pack_pairs(v):
    c = v.shape[1] // 2
    bits = lax.bitcast_convert_type(v.astype(BF16).astype(F32), jnp.uint32)
    return (bits[:, :c] >> 16) | bits[:, c:]


def _unpack_pairs(w):
    lo = lax.bitcast_convert_type(w << 16, F32)
    hi = lax.bitcast_convert_type(w & jnp.uint32(0xFFFF0000), F32)
    return lo, hi


ROW_E1, ROW_E2, ROW_R1_HI, ROW_R1_LO, ROW_R2_HI, ROW_R2_LO = range(6)
ROUTE_ROWS = 8
RANK_RADIX = 256


def _route_rows(x, g, wr, br, carry):
    tm = x.shape[0]
    h = _rms(x, g)
    h_hi = h.astype(BF16)
    h_lo = (h - h_hi.astype(F32)).astype(BF16)
    logits = _dot(jnp.concatenate([h_hi, h_lo, h_hi], axis=1), wr) + br
    lane = lax.broadcasted_iota(jnp.int32, logits.shape, 1)
    neg = jnp.float32(-jnp.inf)
    big = jnp.int32(ROUTE_LANES)

    is_g = (lane >= N_EXPERTS) & (lane < N_EXPERTS + N_GROUPS)
    lg = jnp.where(is_g, logits, neg)
    gmax = jnp.max(lg, axis=-1, keepdims=True)
    g_sel = jnp.min(jnp.where(lg == gmax, lane - N_EXPERTS, big), axis=-1, keepdims=True)
    p_group = 1.0 / jnp.sum(jnp.exp(lg - gmax), axis=-1, keepdims=True)

    in_group = (lane >= g_sel * EXPERTS_PER_GROUP) & (lane < (g_sel + 1) * EXPERTS_PER_GROUP)
    le = jnp.where(in_group, logits, neg)
    v1 = jnp.max(le, axis=-1, keepdims=True)
    i1 = jnp.min(jnp.where(le == v1, lane, big), axis=-1, keepdims=True)
    le2 = jnp.where(lane == i1, neg, le)
    v2 = jnp.max(le2, axis=-1, keepdims=True)
    i2 = jnp.min(jnp.where(le2 == v2, lane, big), axis=-1, keepdims=True)
    t = jnp.exp(v2 - v1)
    g1 = p_group / (1.0 + t)
    g2 = p_group * t / (1.0 + t)

    oh1 = lane == i1
    oh2 = lane == i2
    both = jnp.where(oh1, 1.0, 0.0) + jnp.where(oh2, 1.0, 0.0)
    rr = lax.broadcasted_iota(jnp.int32, (tm, tm), 0)
    cc = lax.broadcasted_iota(jnp.int32, (tm, tm), 1)
    lower = jnp.where(cc < rr, 1.0, 0.0).astype(BF16)
    before = _dot(lower, both.astype(BF16)) + carry
    rank1 = jnp.sum(jnp.where(oh1, before, 0.0), axis=-1, keepdims=True)
    rank2 = jnp.sum(jnp.where(oh2, before, 0.0), axis=-1, keepdims=True)

    gates = jnp.where(lane == 2, g1, 0.0)
    gates = jnp.where(lane == 3, g2, gates)

    r1_hi = jnp.floor(rank1 * (1.0 / RANK_RADIX))
    r2_hi = jnp.floor(rank2 * (1.0 / RANK_RADIX))
    vals = jnp.where(lane == ROW_E1, i1.astype(F32), 0.0)
    vals = jnp.where(lane == ROW_E2, i2.astype(F32), vals)
    vals = jnp.where(lane == ROW_R1_HI, r1_hi, vals)
    vals = jnp.where(lane == ROW_R1_LO, rank1 - r1_hi * RANK_RADIX, vals)
    vals = jnp.where(lane == ROW_R2_HI, r2_hi, vals)
    vals = jnp.where(lane == ROW_R2_LO, rank2 - r2_hi * RANK_RADIX, vals)
    pick = jnp.where(lax.broadcasted_iota(jnp.int32, (ROUTE_ROWS, ROUTE_LANES), 0)
                     == lax.broadcasted_iota(jnp.int32, (ROUTE_ROWS, ROUTE_LANES), 1), 1.0, 0.0).astype(BF16)
    rows = lax.dot_general(pick, vals.astype(BF16), (((1,), (1,)), ((), ())), preferred_element_type=F32)
    return _pack_pairs(h), gates, rows, carry + jnp.sum(both, axis=0, keepdims=True)


def _route_kernel(x_ref, g_ref, wr_ref, br_ref, hp_ref, r_ref, rows_ref, cnt_ref, carry_scr):
    @pl.when(pl.program_id(0) == 0)
    def _():
        carry_scr[...] = jnp.zeros_like(carry_scr)

    hp, gates, row_form, carry = _route_rows(x_ref[...], g_ref[...], wr_ref[...], br_ref[...], carry_scr[...])
    hp_ref[...] = hp
    r_ref[...] = gates
    rows_ref[...] = row_form
    carry_scr[...] = carry
    cnt_ref[...] = carry


def _route(x, g, w_r, b_r, tm):
    n, d = x.shape
    return pl.pallas_call(
        _route_kernel,
        grid=(n // tm,),
        in_specs=[
            pl.BlockSpec((tm, d), lambda i: (i, 0)),
            pl.BlockSpec((1, d), lambda i: (0, 0)),
            pl.BlockSpec(w_r.shape, lambda i: (0, 0)),
            pl.BlockSpec((1, ROUTE_LANES), lambda i: (0, 0)),
        ],
        out_specs=[
            pl.BlockSpec((tm, d // 2), lambda i: (i, 0)),
            pl.BlockSpec((tm, ROUTE_LANES), lambda i: (i, 0)),
            pl.BlockSpec((ROUTE_ROWS, tm), lambda i: (0, i)),
            pl.BlockSpec((1, ROUTE_LANES), lambda i: (0, 0)),
        ],
        out_shape=[
            jax.ShapeDtypeStruct((n, d // 2), jnp.uint32),
            jax.ShapeDtypeStruct((n, ROUTE_LANES), F32),
            jax.ShapeDtypeStruct((ROUTE_ROWS, n), F32),
            jax.ShapeDtypeStruct((1, ROUTE_LANES), F32),
        ],
        scratch_shapes=[pltpu.VMEM((1, ROUTE_LANES), F32)],
        compiler_params=_params("arbitrary"),
        name="route",
    )(x, g, w_r, b_r)


def _moe_plan(rows, cnt, tm):
    n = rows.shape[1]
    ri = rows.astype(jnp.int32)
    experts = jnp.arange(N_EXPERTS, dtype=jnp.int32)
    counts = cnt[0, :N_EXPERTS].astype(jnp.int32)
    padded = (counts + tm - 1) // tm * tm
    pends = jnp.sum(jnp.where(experts[None, :] <= experts[:, None], padded[None, :], 0), axis=1)
    pstarts = pends - padded

    def dest(e, hi, lo):
        start = jnp.sum(jnp.where(e[:, None] == experts[None, :], pstarts[None, :], 0), axis=1)
        return start + hi * RANK_RADIX + lo

    dest_flat = jnp.concatenate([dest(ri[ROW_E1], ri[ROW_R1_HI], ri[ROW_R1_LO]),
                                 dest(ri[ROW_E2], ri[ROW_R2_HI], ri[ROW_R2_LO])])
    n_tiles = (2 * n + N_EXPERTS * tm) // tm
    n_used = (pends[-1] // tm).astype(jnp.int32)
    tile_start = jnp.arange(n_tiles, dtype=jnp.int32) * tm
    te = jnp.sum((pends[None, :] <= tile_start[:, None]).astype(jnp.int32), axis=1)
    last_used = jnp.max(jnp.where(counts > 0, experts, 0))
    te = jnp.minimum(te, last_used)
    return dest_flat, counts, pends, te, n_used.reshape(1)


def _dispatch_kernel(dest_ref, cnt_ref, pend_ref, h_ref, xp_hbm, zero_scr, sem, zsem, *, tm, tile, n):
    i = pl.program_id(0)

    @pl.when(i == 0)
    def _():
        zero_scr[...] = jnp.zeros_like(zero_scr)

        def zero_tile(t, carry):
            cp = pltpu.make_async_copy(zero_scr, xp_hbm.at[pl.ds(pl.multiple_of(t * tile, tile), tile)], zsem)
            cp.start()
            cp.wait()
            return carry

        lax.fori_loop(pend_ref[N_EXPERTS - 1] // tile, xp_hbm.shape[0] // tile, zero_tile, 0)
        for e in range(N_EXPERTS):
            @pl.when(cnt_ref[e] > 0)
            def _():
                start = pl.multiple_of(pend_ref[e] - tile, tile)
                pltpu.make_async_copy(zero_scr, xp_hbm.at[pl.ds(start, tile)], zsem).start()
        for e in range(N_EXPERTS):
            @pl.when(cnt_ref[e] > 0)
            def _():
                start = pl.multiple_of(pend_ref[e] - tile, tile)
                pltpu.make_async_copy(zero_scr, xp_hbm.at[pl.ds(start, tile)], zsem).wait()

    base = i * tm

    def row_copy(r, k):
        return pltpu.make_async_copy(h_ref.at[pl.ds(r, 1)], xp_hbm.at[pl.ds(dest_ref[k * n + base + r], 1)], sem)

    copies = [row_copy(r, k) for r in range(tm) for k in range(2)]
    for ci, cp in enumerate(copies):
        cp.start(priority=ci % ROW_DMA_QUEUES)
    for cp in copies:
        cp.wait()


def _dispatch(dest_flat, counts, pends, hp, tm, tile):
    n, c = hp.shape
    rows = 2 * n + N_EXPERTS * tile
    grid_spec = pltpu.PrefetchScalarGridSpec(
        num_scalar_prefetch=3,
        grid=(n // tm,),
        in_specs=[pl.BlockSpec((tm, c), lambda i, *_: (i, 0))],
        out_specs=pl.BlockSpec(memory_space=pl.ANY),
        scratch_shapes=[pltpu.VMEM((tile, c), jnp.uint32), pltpu.SemaphoreType.DMA, pltpu.SemaphoreType.DMA],
    )
    return pl.pallas_call(
        functools.partial(_dispatch_kernel, tm=tm, tile=tile, n=n),
        grid_spec=grid_spec,
        out_shape=jax.ShapeDtypeStruct((rows, c), jnp.uint32),
        compiler_params=_params("arbitrary"),
        name="dispatch",
    )(dest_flat, counts, pends, hp)


N_WGROUPS = 8
N_WBUF = 2


def _gmm_plan(te, n_used, counts, pends, tm):
    n_tiles = te.shape[0]
    idx = jnp.arange(n_tiles, dtype=jnp.int32)
    experts = jnp.arange(N_EXPERTS, dtype=jnp.int32)
    by_tile = lambda table: jnp.sum(jnp.where(te[:, None] == experts[None, :], table[None, :], 0), axis=1)
    run_len = by_tile((counts + tm - 1) // tm)
    pos = idx - by_tile(pends // tm - (counts + tm - 1) // tm)
    later = (experts[None, :] > experts[:, None]) & (counts[None, :] > 0)
    nxt_e = jnp.min(jnp.where(later, experts[None, :], N_EXPERTS), axis=1)
    nxt = by_tile(jnp.where(nxt_e < N_EXPERTS, nxt_e, -1))
    used_before = jnp.sum(jnp.where((experts[None, :] < experts[:, None]) & (counts[None, :] > 0), 1, 0), axis=1)
    slot = by_tile(used_before % 2)
    safe_len = jnp.maximum(run_len, 1)
    g0 = pos * N_WGROUPS // safe_len
    kk = (pos + 1) * N_WGROUPS // safe_len - g0
    first = (pos == 0).astype(jnp.int32)
    fix = lambda v: jnp.where(idx < n_used, v, 0).astype(jnp.int32)
    return fix(nxt) - (idx >= n_used).astype(jnp.int32), fix(slot), fix(first), fix(g0), fix(kk)


def _gmm_kernel(te_ref, nu_ref, nxt_ref, slot_ref, first_ref, g0_ref, kk_ref, x_ref, wg_hbm, wu_hbm, wd_hbm, y_ref,
                wg_s, wu_s, wd_s, stg_g, stg_u, stg_d, sem, *, l):
    i = pl.program_id(0)
    used = i < nu_ref[0]
    ra = wg_s.shape[1] // N_WGROUPS
    rd = wd_s.shape[1] // N_WGROUPS

    def group_copies(e, g):
        b = g % N_WBUF
        a0 = pl.multiple_of(g * ra, ra)
        d0 = pl.multiple_of(g * rd, rd)
        return (pltpu.make_async_copy(wg_hbm.at[l, e, pl.ds(a0, ra)], stg_g.at[b], sem.at[0, b]),
                pltpu.make_async_copy(wu_hbm.at[l, e, pl.ds(a0, ra)], stg_u.at[b], sem.at[1, b]),
                pltpu.make_async_copy(wd_hbm.at[l, e, pl.ds(d0, rd)], stg_d.at[b], sem.at[2, b]))

    def start_group(e, g):
        for cp in group_copies(e, g):
            cp.start()

    def begin_load(e):
        for g in range(N_WBUF):
            start_group(e, g)

    def cast_groups(e, slot, first_g, n_g):
        def body(it, carry):
            g = first_g + it
            b = g % N_WBUF
            for cp in group_copies(e, g):
                cp.wait()
            a0 = pl.multiple_of(g * ra, ra)
            d0 = pl.multiple_of(g * rd, rd)
            wg_s[slot, pl.ds(a0, ra), :] = stg_g[b].astype(BF16)
            wu_s[slot, pl.ds(a0, ra), :] = stg_u[b].astype(BF16)
            wd_s[slot, pl.ds(d0, rd), :] = stg_d[b].astype(BF16)

            @pl.when(g + N_WBUF < N_WGROUPS)
            def _():
                start_group(e, g + N_WBUF)
            return carry
        lax.fori_loop(0, n_g, body, 0)

    @pl.when(used)
    def _():
        slot = slot_ref[i]

        @pl.when(i == 0)
        def _():
            begin_load(te_ref[0])
            cast_groups(te_ref[0], slot, 0, N_WGROUPS)

        nxt = nxt_ref[i]

        @pl.when(nxt >= 0)
        def _():
            @pl.when(first_ref[i] == 1)
            def _():
                begin_load(nxt)
            cast_groups(nxt, 1 - slot, g0_ref[i], kk_ref[i])

        lo, hi = _unpack_pairs(x_ref[...])
        x = jnp.concatenate([lo.astype(BF16), hi.astype(BF16)], axis=1)
        a = _dot(x, wg_s[slot])
        b = _dot(x, wu_s[slot])
        hid = (a * jax.nn.sigmoid(a) * b).astype(BF16)
        y_ref[...] = _pack_pairs(_dot(hid, wd_s[slot]))

    @pl.when(jnp.logical_not(used))
    def _():
        y_ref[...] = jnp.zeros_like(y_ref)


def _gmm(tile_expert, n_used, plan, x_pad, w_gate, w_up, w_down, l, tm):
    rows, c = x_pad.shape
    _, _, d, de = w_gate.shape

    def row_map(i, te, nu, *_):
        return (jnp.minimum(i, jnp.maximum(nu[0] - 1, 0)), 0)

    grid_spec = pltpu.PrefetchScalarGridSpec(
        num_scalar_prefetch=2 + len(plan),
        grid=(rows // tm,),
        in_specs=[
            pl.BlockSpec((tm, c), row_map),
            pl.BlockSpec(memory_space=pl.ANY),
            pl.BlockSpec(memory_space=pl.ANY),
            pl.BlockSpec(memory_space=pl.ANY),
        ],
        out_specs=pl.BlockSpec((tm, c), lambda i, *_: (i, 0)),
        scratch_shapes=[
            pltpu.VMEM((2, d, de), BF16),
            pltpu.VMEM((2, d, de), BF16),
            pltpu.VMEM((2, de, d), BF16),
            pltpu.VMEM((N_WBUF, d // N_WGROUPS, de), F32),
            pltpu.VMEM((N_WBUF, d // N_WGROUPS, de), F32),
            pltpu.VMEM((N_WBUF, de // N_WGROUPS, d), F32),
            pltpu.SemaphoreType.DMA((3, N_WBUF)),
        ],
    )
    return pl.pallas_call(
        functools.partial(_gmm_kernel, l=l),
        grid_spec=grid_spec,
        out_shape=jax.ShapeDtypeStruct((rows, c), jnp.uint32),
        compiler_params=_params("arbitrary"),
        name="gmm",
    )(tile_expert, n_used, *plan, x_pad, w_gate, w_up, w_down)


def _combine_kernel(dest_ref, x_ref, r_ref, g_ref, yp_hbm, o_ref, ybuf, sem, *, tm, n, final_norm):
    j = pl.program_id(0)
    n_tiles = n // tm
    c = ybuf.shape[-1]

    def row_copy(src_row, slot, r, k):
        return pltpu.make_async_copy(yp_hbm.at[pl.ds(src_row, 1)], ybuf.at[slot, k, pl.ds(r, 1)], sem.at[slot])

    def start_gather(slot):
        for r in range(tm):
            for k in range(2):
                row_copy(dest_ref[k * n + j * tm + r], slot, r, k).start(priority=(2 * r + k) % ROW_DMA_QUEUES)

    def finish_tile(slot):
        for r in range(tm):
            for k in range(2):
                row_copy(0, slot, r, k).wait()
        g1 = r_ref[:, 2:3]
        g2 = r_ref[:, 3:4]
        lo1, hi1 = _unpack_pairs(ybuf[slot, 0])
        lo2, hi2 = _unpack_pairs(ybuf[slot, 1])
        x = x_ref[...]
        out = jnp.concatenate([x[:, :c] + g1 * lo1 + g2 * lo2, x[:, c:] + g1 * hi1 + g2 * hi2], axis=1)
        o_ref[...] = _rms(out, g_ref[...]) if final_norm else out

    for slot in range(2):
        pl.when((j < n_tiles) & (j % 2 == slot))(functools.partial(start_gather, slot))
    for slot in range(2):
        pl.when((j > 0) & ((j - 1) % 2 == slot))(functools.partial(finish_tile, slot))


def _combine(dest_flat, x, route, g, y_pad, tm, final_norm):
    n, d = x.shape
    c = y_pad.shape[1]
    grid_spec = pltpu.PrefetchScalarGridSpec(
        num_scalar_prefetch=1,
        grid=(n // tm + 1,),
        in_specs=[
            pl.BlockSpec((tm, d), lambda j, *_: (jnp.maximum(j - 1, 0), 0)),
            pl.BlockSpec((tm, ROUTE_LANES), lambda j, *_: (jnp.maximum(j - 1, 0), 0)),
            pl.BlockSpec((1, d), lambda j, *_: (0, 0)),
            pl.BlockSpec(memory_space=pl.ANY),
        ],
        out_specs=pl.BlockSpec((tm, d), lambda j, *_: (jnp.maximum(j - 1, 0), 0)),
        scratch_shapes=[pltpu.VMEM((2, 2, tm, c), jnp.uint32), pltpu.SemaphoreType.DMA((2,))],
    )
    return pl.pallas_call(
        functools.partial(_combine_kernel, tm=tm, n=n, final_norm=final_norm),
        grid_spec=grid_spec,
        out_shape=jax.ShapeDtypeStruct((n, d), F32),
        compiler_params=_params("arbitrary"),
        name="combine",
    )(dest_flat, x, route, g, y_pad)


def _tile(n, want):
    return min(n, want)


def kernel(x, mem, w_in, pool_mix, pool_scale, w_pool_out, conv_w, w_conv_out, w_mix_out, g_mix, g_xattn, g_mem, w_q, w_k, w_v, w_o, g_ffn, w_route_group, b_route_group, w_route_expert, b_route_expert, w_gate, w_up, w_down, g_final):
    bsz, seq, d = x.shape
    n_mem = mem.shape[1]
    depth = w_in.shape[0]
    n = bsz * seq
    xf = x.reshape(n, d)
    memf = mem.reshape(bsz * n_mem, d)

    tm_in = _tile(seq, 1024)
    tm_mid = _tile(seq, 256)
    tm_att = _tile(seq, 512)
    tm_route = _tile(seq, 1024)
    tm_disp = _tile(seq, 1024)
    tm_gmm = 256
    tm_comb = _tile(seq, 512)

    bf = lambda w: w.astype(BF16)
    w_in, pool_mix, w_pool_out, w_conv_out, w_mix_out = map(bf, (w_in, pool_mix, w_pool_out, w_conv_out, w_mix_out))
    w_q, w_k, w_v, w_o = map(bf, (w_q, w_k, w_v, w_o))
    pool_scale = pool_scale.reshape(depth, 1, -1)

    pad_lanes = ROUTE_LANES - N_EXPERTS - N_GROUPS
    for l in range(depth):
        row = lambda v: v[l].reshape(1, -1)
        z = _mixer_in(xf, row(g_mix), w_in, l, tm_in, _tile(w_in.shape[2], 2048))
        xf = _mixer_mid(xf, z, pool_mix, pool_scale, w_pool_out, conv_w, w_conv_out, w_mix_out, l, seq, tm_mid)

        kv = _kv_proj(memf, row(g_mem), w_k, w_v, l, _tile(d, 1024))
        w_r = jnp.concatenate([w_route_expert[l], w_route_group[l], jnp.zeros((d, pad_lanes), F32)], axis=1)
        w_r_hi = w_r.astype(BF16)
        w_r_lo = (w_r - w_r_hi.astype(F32)).astype(BF16)
        w_r3 = jnp.concatenate([w_r_hi, w_r_hi, w_r_lo], axis=0)
        b_r = jnp.concatenate([b_route_expert[l], b_route_group[l], jnp.zeros((pad_lanes,), F32)]).reshape(1, -1)
        xf = _xattn(xf, row(g_xattn), w_q, kv, w_o, l, seq, n_mem, tm_att)
        hp, route, rows, cnt = _route(xf, row(g_ffn), w_r3, b_r, tm_route)
        dest_flat, counts, pends, te, n_used = _moe_plan(rows, cnt, tm_gmm)
        x_pad = _dispatch(dest_flat, counts, pends, hp, tm_disp, tm_gmm)
        plan = _gmm_plan(te, n_used, counts, pends, tm_gmm)
        y_pad = _gmm(te, n_used, plan, x_pad, w_gate, w_up, w_down, l, tm_gmm)
        xf = _combine(dest_flat, xf, route, g_final.reshape(1, -1), y_pad, tm_comb, final_norm=(l == depth - 1))
    return xf.reshape(bsz, seq, d)
```

```python
import functools
import math

import jax
import jax.numpy as jnp
from jax import lax
from jax.experimental import pallas as pl
from jax.experimental.pallas import tpu as pltpu

EPS = 1e-6
POOL_WINDOWS = (2, 4, 8, 16)
CONV_K = 3
XA_HEADS = 4
N_GROUPS = 4
EXPERTS_PER_GROUP = 8
N_EXPERTS = N_GROUPS * EXPERTS_PER_GROUP
HALO = 16
ROW_DMA_QUEUES = 2
ROUTE_LANES = 128
VMEM_LIMIT = 56 * 1024 * 1024

F32 = jnp.float32
BF16 = jnp.bfloat16


def _params(*sem):
    return pltpu.CompilerParams(dimension_semantics=sem, vmem_limit_bytes=VMEM_LIMIT)


def _resident(stack, l):
    nd = stack.ndim - 1
    return pl.BlockSpec((None,) + stack.shape[1:], lambda *_: (l,) + (0,) * nd, pipeline_mode=pl.Buffered(1))


def _rms(xf, g):
    ms = jnp.mean(xf * xf, axis=-1, keepdims=True)
    return xf * lax.rsqrt(ms + EPS) * g


def _dot(a, b):
    return jnp.dot(a, b, preferred_element_type=F32)


def _mixer_in_kernel(x_ref, g_ref, w_ref, z_ref, h_scr):
    @pl.when(pl.program_id(1) == 0)
    def _():
        h_scr[...] = _rms(x_ref[...], g_ref[...]).astype(BF16)

    z_ref[...] = _dot(h_scr[...], w_ref[...]).astype(z_ref.dtype)


def _mixer_in(x, g, w, l, tm, tn):
    n, d = x.shape
    c = w.shape[2]
    return pl.pallas_call(
        _mixer_in_kernel,
        grid=(n // tm, c // tn),
        in_specs=[
            pl.BlockSpec((tm, d), lambda i, j: (i, 0)),
            pl.BlockSpec((1, d), lambda i, j: (0, 0)),
            pl.BlockSpec((None, d, tn), lambda i, j: (l, 0, j)),
        ],
        out_specs=pl.BlockSpec((tm, tn), lambda i, j: (i, j)),
        out_shape=jax.ShapeDtypeStruct((n, c), BF16),
        scratch_shapes=[pltpu.VMEM((tm, d), BF16)],
        compiler_params=_params("arbitrary", "arbitrary"),
        name="mixer_in",
    )(x, g, w)


def _mixer_mid_kernel(x_ref, z_ref, zh_ref, pm_ref, ps_ref, wpo_ref, cw_ref, wco_ref, wmo_ref, o_ref,
                      *, tiles_per_seq, tm, d):
    pw = d // 2
    gw = pw // len(POOL_WINDOWS)
    t = pl.program_id(0) % tiles_per_seq
    keep = jnp.where(t == 0, 0.0, 1.0).astype(F32)
    pos = lax.broadcasted_iota(jnp.int32, (tm, gw), 0) + (t * tm + 1)

    uc = z_ref[:, pw:2 * pw].astype(F32)
    bg = z_ref[:, 2 * pw:3 * pw].astype(F32)
    cg = z_ref[:, 3 * pw:4 * pw].astype(F32)
    v = cg * uc
    vh = zh_ref[:, 3 * pw:4 * pw].astype(F32) * zh_ref[:, pw:2 * pw].astype(F32) * keep
    vext = jnp.concatenate([vh, v], axis=0)
    y = cw_ref[CONV_K - 1:CONV_K, :] * v
    for j in range(CONV_K - 1):
        y = y + cw_ref[j:j + 1, :] * pltpu.roll(vext, CONV_K - 1 - j, axis=0)[HALO:, :]
    br_b = _dot((bg * y).astype(BF16), wco_ref[...])
    gated_b = jax.nn.sigmoid(z_ref[:, 3 * d:4 * d].astype(F32)) * br_b

    parts = []
    for gi, w in enumerate(POOL_WINDOWS):
        cols = slice(gi * gw, (gi + 1) * gw)
        u = z_ref[:, cols].astype(F32)
        ext = jnp.concatenate([zh_ref[:, cols].astype(F32) * keep, u], axis=0)
        s = 1
        while s < w:
            ext = ext + pltpu.roll(ext, s, axis=0)
            s *= 2
        cnt = jnp.minimum(pos, w).astype(F32)
        p = (ext[HALO:, :] / cnt - u).astype(BF16)
        m = _dot(p, pm_ref[gi]) * ps_ref[:, cols]
        parts.append(m.astype(BF16))
    br_a = _dot(jnp.concatenate(parts, axis=1), wpo_ref[...])

    merged = jax.nn.sigmoid(z_ref[:, 2 * d:3 * d].astype(F32)) * br_a + gated_b
    o_ref[...] = x_ref[...] + _dot(merged.astype(BF16), wmo_ref[...])


def _mixer_mid(x, z, pool_mix, pool_scale, w_pool_out, conv_w, w_conv_out, w_mix_out, l, seq, tm):
    n, d = x.shape
    c = z.shape[1]
    hb = tm // HALO
    kern = functools.partial(_mixer_mid_kernel, tiles_per_seq=seq // tm, tm=tm, d=d)
    return pl.pallas_call(
        kern,
        grid=(n // tm,),
        in_specs=[
            pl.BlockSpec((tm, d), lambda i: (i, 0)),
            pl.BlockSpec((tm, c), lambda i: (i, 0)),
            pl.BlockSpec((HALO, c // 2), lambda i: (jnp.maximum(i * hb - 1, 0), 0)),
            _resident(pool_mix, l),
            _resident(pool_scale, l),
            _resident(w_pool_out, l),
            _resident(conv_w, l),
            _resident(w_conv_out, l),
            _resident(w_mix_out, l),
        ],
        out_specs=pl.BlockSpec((tm, d), lambda i: (i, 0)),
        out_shape=jax.ShapeDtypeStruct((n, d), F32),
        compiler_params=_params("arbitrary"),
        name="mixer_mid",
    )(x, z, z, pool_mix, pool_scale, w_pool_out, conv_w, w_conv_out, w_mix_out)


def _kv_kernel(m_ref, g_ref, wk_ref, wv_ref, o_ref, *, nb):
    mn = _rms(m_ref[...], g_ref[...]).astype(BF16)
    j = pl.program_id(0)

    @pl.when(j < nb)
    def _():
        o_ref[...] = _dot(mn, wk_ref[...]).astype(o_ref.dtype)

    @pl.when(j >= nb)
    def _():
        o_ref[...] = _dot(mn, wv_ref[...]).astype(o_ref.dtype)


def _kv_proj(mem, g, w_k, w_v, l, tn):
    r, d = mem.shape
    nb = w_k.shape[2] // tn
    return pl.pallas_call(
        functools.partial(_kv_kernel, nb=nb),
        grid=(2 * nb,),
        in_specs=[
            pl.BlockSpec((r, d), lambda j: (0, 0)),
            pl.BlockSpec((1, d), lambda j: (0, 0)),
            pl.BlockSpec((None, d, tn), lambda j: (l, 0, jnp.minimum(j, nb - 1))),
            pl.BlockSpec((None, d, tn), lambda j: (l, 0, jnp.maximum(j - nb, 0))),
        ],
        out_specs=pl.BlockSpec((r, tn), lambda j: (0, j)),
        out_shape=jax.ShapeDtypeStruct((r, 2 * nb * tn), BF16),
        compiler_params=_params("arbitrary"),
        name="kv_proj",
    )(mem, g, w_k, w_v)


def _attend_rows(x, g, wq_ref, kv_ref, wo_ref, d):
    hd = d // XA_HEADS
    scale = 1.0 / math.sqrt(hd)
    q = _dot(_rms(x, g).astype(BF16), wq_ref[...]).astype(BF16)
    heads = []
    for h in range(XA_HEADS):
        qh = q[:, h * hd:(h + 1) * hd]
        kh = kv_ref[:, h * hd:(h + 1) * hd]
        vh = kv_ref[:, d + h * hd:d + (h + 1) * hd]
        s = lax.dot_general(qh, kh, (((1,), (1,)), ((), ())), preferred_element_type=F32) * scale
        e = jnp.exp(s - jnp.max(s, axis=-1, keepdims=True))
        p = (e / jnp.sum(e, axis=-1, keepdims=True)).astype(BF16)
        heads.append(_dot(p, vh).astype(BF16))
    return x + _dot(jnp.concatenate(heads, axis=1), wo_ref[...])


def _xattn_kernel(x_ref, g_ref, wq_ref, kv_ref, wo_ref, o_ref, *, d):
    o_ref[...] = _attend_rows(x_ref[...], g_ref[...], wq_ref, kv_ref, wo_ref, d)


def _xattn(x, g, w_q, kv, w_o, l, seq, n_mem, tm):
    n, d = x.shape
    tiles_per_seq = seq // tm
    return pl.pallas_call(
        functools.partial(_xattn_kernel, d=d),
        grid=(n // tm,),
        in_specs=[
            pl.BlockSpec((tm, d), lambda i: (i, 0)),
            pl.BlockSpec((1, d), lambda i: (0, 0)),
            _resident(w_q, l),
            pl.BlockSpec((n_mem, 2 * d), lambda i: (i // tiles_per_seq, 0)),
            _resident(w_o, l),
        ],
        out_specs=pl.BlockSpec((tm, d), lambda i: (i, 0)),
        out_shape=jax.ShapeDtypeStruct((n, d), F32),
        compiler_params=_params("arbitrary"),
        name="xattn",
    )(x, g, w_q, kv, w_o)


def _pack_pairs(v):
    c = v.shape[1] // 2
    bits = lax.bitcast_convert_type(v.astype(BF16).astype(F32), jnp.uint32)
    return (bits[:, :c] >> 16) | bits[:, c:]


def _unpack_pairs(w):
    lo = lax.bitcast_convert_type(w << 16, F32)
    hi = lax.bitcast_convert_type(w & jnp.uint32(0xFFFF0000), F32)
    return lo, hi


ROW_E1, ROW_E2, ROW_R1_HI, ROW_R1_LO, ROW_R2_HI, ROW_R2_LO = range(6)
ROUTE_ROWS = 8
RANK_RADIX = 256


def _route_rows(x, g, wr, br, carry):
    tm = x.shape[0]
    h = _rms(x, g)
    h_hi = h.astype(BF16)
    h_lo = (h - h_hi.astype(F32)).astype(BF16)
    logits = _dot(jnp.concatenate([h_hi, h_lo, h_hi], axis=1), wr) + br
    lane = lax.broadcasted_iota(jnp.int32, logits.shape, 1)
    neg = jnp.float32(-jnp.inf)
    big = jnp.int32(ROUTE_LANES)

    is_g = (lane >= N_EXPERTS) & (lane < N_EXPERTS + N_GROUPS)
    lg = jnp.where(is_g, logits, neg)
    gmax = jnp.max(lg, axis=-1, keepdims=True)
    g_sel = jnp.min(jnp.where(lg == gmax, lane - N_EXPERTS, big), axis=-1, keepdims=True)
    p_group = 1.0 / jnp.sum(jnp.exp(lg - gmax), axis=-1, keepdims=True)

    in_group = (lane >= g_sel * EXPERTS_PER_GROUP) & (lane < (g_sel + 1) * EXPERTS_PER_GROUP)
    le = jnp.where(in_group, logits, neg)
    v1 = jnp.max(le, axis=-1, keepdims=True)
    i1 = jnp.min(jnp.where(le == v1, lane, big), axis=-1, keepdims=True)
    le2 = jnp.where(lane == i1, neg, le)
    v2 = jnp.max(le2, axis=-1, keepdims=True)
    i2 = jnp.min(jnp.where(le2 == v2, lane, big), axis=-1, keepdims=True)
    t = jnp.exp(v2 - v1)
    g1 = p_group / (1.0 + t)
    g2 = p_group * t / (1.0 + t)

    oh1 = lane == i1
    oh2 = lane == i2
    both = jnp.where(oh1, 1.0, 0.0) + jnp.where(oh2, 1.0, 0.0)
    rr = lax.broadcasted_iota(jnp.int32, (tm, tm), 0)
    cc = lax.broadcasted_iota(jnp.int32, (tm, tm), 1)
    lower = jnp.where(cc < rr, 1.0, 0.0).astype(BF16)
    before = _dot(lower, both.astype(BF16)) + carry
    rank1 = jnp.sum(jnp.where(oh1, before, 0.0), axis=-1, keepdims=True)
    rank2 = jnp.sum(jnp.where(oh2, before, 0.0), axis=-1, keepdims=True)

    gates = jnp.where(lane == 2, g1, 0.0)
    gates = jnp.where(lane == 3, g2, gates)

    r1_hi = jnp.floor(rank1 * (1.0 / RANK_RADIX))
    r2_hi = jnp.floor(rank2 * (1.0 / RANK_RADIX))
    vals = jnp.where(lane == ROW_E1, i1.astype(F32), 0.0)
    vals = jnp.where(lane == ROW_E2, i2.astype(F32), vals)
    vals = jnp.where(lane == ROW_R1_HI, r1_hi, vals)
    vals = jnp.where(lane == ROW_R1_LO, rank1 - r1_hi * RANK_RADIX, vals)
    vals = jnp.where(lane == ROW_R2_HI, r2_hi, vals)
    vals = jnp.where(lane == ROW_R2_LO, rank2 - r2_hi * RANK_RADIX, vals)
    pick = jnp.where(lax.broadcasted_iota(jnp.int32, (ROUTE_ROWS, ROUTE_LANES), 0)
                     == lax.broadcasted_iota(jnp.int32, (ROUTE_ROWS, ROUTE_LANES), 1), 1.0, 0.0).astype(BF16)
    rows = lax.dot_general(pick, vals.astype(BF16), (((1,), (1,)), ((), ())), preferred_element_type=F32)
    return _pack_pairs(h), gates, rows, carry + jnp.sum(both, axis=0, keepdims=True)


def _route_kernel(x_ref, g_ref, wr_ref, br_ref, hp_ref, r_ref, rows_ref, cnt_ref, carry_scr):
    @pl.when(pl.program_id(0) == 0)
    def _():
        carry_scr[...] = jnp.zeros_like(carry_scr)

    hp, gates, row_form, carry = _route_rows(x_ref[...], g_ref[...], wr_ref[...], br_ref[...], carry_scr[...])
    hp_ref[...] = hp
    r_ref[...] = gates
    rows_ref[...] = row_form
    carry_scr[...] = carry
    cnt_ref[...] = carry


def _route(x, g, w_r, b_r, tm):
    n, d = x.shape
    return pl.pallas_call(
        _route_kernel,
        grid=(n // tm,),
        in_specs=[
            pl.BlockSpec((tm, d), lambda i: (i, 0)),
            pl.BlockSpec((1, d), lambda i: (0, 0)),
            pl.BlockSpec(w_r.shape, lambda i: (0, 0)),
            pl.BlockSpec((1, ROUTE_LANES), lambda i: (0, 0)),
        ],
        out_specs=[
            pl.BlockSpec((tm, d // 2), lambda i: (i, 0)),
            pl.BlockSpec((tm, ROUTE_LANES), lambda i: (i, 0)),
            pl.BlockSpec((ROUTE_ROWS, tm), lambda i: (0, i)),
            pl.BlockSpec((1, ROUTE_LANES), lambda i: (0, 0)),
        ],
        out_shape=[
            jax.ShapeDtypeStruct((n, d // 2), jnp.uint32),
            jax.ShapeDtypeStruct((n, ROUTE_LANES), F32),
            jax.ShapeDtypeStruct((ROUTE_ROWS, n), F32),
            jax.ShapeDtypeStruct((1, ROUTE_LANES), F32),
        ],
        scratch_shapes=[pltpu.VMEM((1, ROUTE_LANES), F32)],
        compiler_params=_params("arbitrary"),
        name="route",
    )(x, g, w_r, b_r)


def _moe_plan(rows, cnt, tm):
    n = rows.shape[1]
    ri = rows.astype(jnp.int32)
    experts = jnp.arange(N_EXPERTS, dtype=jnp.int32)
    counts = cnt[0, :N_EXPERTS].astype(jnp.int32)
    padded = (counts + tm - 1) // tm * tm
    pends = jnp.sum(jnp.where(experts[None, :] <= experts[:, None], padded[None, :], 0), axis=1)
    pstarts = pends - padded

    def dest(e, hi, lo):
        start = jnp.sum(jnp.where(e[:, None] == experts[None, :], pstarts[None, :], 0), axis=1)
        return start + hi * RANK_RADIX + lo

    dest_flat = jnp.concatenate([dest(ri[ROW_E1], ri[ROW_R1_HI], ri[ROW_R1_LO]),
                                 dest(ri[ROW_E2], ri[ROW_R2_HI], ri[ROW_R2_LO])])
    n_tiles = (2 * n + N_EXPERTS * tm) // tm
    n_used = (pends[-1] // tm).astype(jnp.int32)
    tile_start = jnp.arange(n_tiles, dtype=jnp.int32) * tm
    te = jnp.sum((pends[None, :] <= tile_start[:, None]).astype(jnp.int32), axis=1)
    last_used = jnp.max(jnp.where(counts > 0, experts, 0))
    te = jnp.minimum(te, last_used)
    return dest_flat, counts, pends, te, n_used.reshape(1)


def _dispatch_kernel(dest_ref, cnt_ref, pend_ref, h_ref, xp_hbm, zero_scr, sem, zsem, *, tm, tile, n):
    i = pl.program_id(0)

    @pl.when(i == 0)
    def _():
        zero_scr[...] = jnp.zeros_like(zero_scr)

        def zero_tile(t, carry):
            cp = pltpu.make_async_copy(zero_scr, xp_hbm.at[pl.ds(pl.multiple_of(t * tile, tile), tile)], zsem)
            cp.start()
            cp.wait()
            return carry

        lax.fori_loop(pend_ref[N_EXPERTS - 1] // tile, xp_hbm.shape[0] // tile, zero_tile, 0)
        for e in range(N_EXPERTS):
            @pl.when(cnt_ref[e] > 0)
            def _():
                start = pl.multiple_of(pend_ref[e] - tile, tile)
                pltpu.make_async_copy(zero_scr, xp_hbm.at[pl.ds(start, tile)], zsem).start()
        for e in range(N_EXPERTS):
            @pl.when(cnt_ref[e] > 0)
            def _():
                start = pl.multiple_of(pend_ref[e] - tile, tile)
                pltpu.make_async_copy(zero_scr, xp_hbm.at[pl.ds(start, tile)], zsem).wait()

    base = i * tm

    def row_copy(r, k):
        return pltpu.make_async_copy(h_ref.at[pl.ds(r, 1)], xp_hbm.at[pl.ds(dest_ref[k * n + base + r], 1)], sem)

    copies = [row_copy(r, k) for r in range(tm) for k in range(2)]
    for ci, cp in enumerate(copies):
        cp.start(priority=ci % ROW_DMA_QUEUES)
    for cp in copies:
        cp.wait()


def _dispatch(dest_flat, counts, pends, hp, tm, tile):
    n, c = hp.shape
    rows = 2 * n + N_EXPERTS * tile
    grid_spec = pltpu.PrefetchScalarGridSpec(
        num_scalar_prefetch=3,
        grid=(n // tm,),
        in_specs=[pl.BlockSpec((tm, c), lambda i, *_: (i, 0))],
        out_specs=pl.BlockSpec(memory_space=pl.ANY),
        scratch_shapes=[pltpu.VMEM((tile, c), jnp.uint32), pltpu.SemaphoreType.DMA, pltpu.SemaphoreType.DMA],
    )
    return pl.pallas_call(
        functools.partial(_dispatch_kernel, tm=tm, tile=tile, n=n),
        grid_spec=grid_spec,
        out_shape=jax.ShapeDtypeStruct((rows, c), jnp.uint32),
        compiler_params=_params("arbitrary"),
        name="dispatch",
    )(dest_flat, counts, pends, hp)


N_WGROUPS = 8
N_WBUF = 2


def _gmm_plan(te, n_used, counts, pends, tm):
    n_tiles = te.shape[0]
    idx = jnp.arange(n_tiles, dtype=jnp.int32)
    experts = jnp.arange(N_EXPERTS, dtype=jnp.int32)
    by_tile = lambda table: jnp.sum(jnp.where(te[:, None] == experts[None, :], table[None, :], 0), axis=1)
    run_len = by_tile((counts + tm - 1) // tm)
    pos = idx - by_tile(pends // tm - (counts + tm - 1) // tm)
    later = (experts[None, :] > experts[:, None]) & (counts[None, :] > 0)
    nxt_e = jnp.min(jnp.where(later, experts[None, :], N_EXPERTS), axis=1)
    nxt = by_tile(jnp.where(nxt_e < N_EXPERTS, nxt_e, -1))
    used_before = jnp.sum(jnp.where((experts[None, :] < experts[:, None]) & (counts[None, :] > 0), 1, 0), axis=1)
    slot = by_tile(used_before % 2)
    safe_len = jnp.maximum(run_len, 1)
    g0 = pos * N_WGROUPS // safe_len
    kk = (pos + 1) * N_WGROUPS // safe_len - g0
    first = (pos == 0).astype(jnp.int32)
    fix = lambda v: jnp.where(idx < n_used, v, 0).astype(jnp.int32)
    return fix(nxt) - (idx >= n_used).astype(jnp.int32), fix(slot), fix(first), fix(g0), fix(kk)


def _gmm_kernel(te_ref, nu_ref, nxt_ref, slot_ref, first_ref, g0_ref, kk_ref, x_ref, wg_hbm, wu_hbm, wd_hbm, y_ref,
                wg_s, wu_s, wd_s, stg_g, stg_u, stg_d, sem, *, l):
    i = pl.program_id(0)
    used = i < nu_ref[0]
    ra = wg_s.shape[1] // N_WGROUPS
    rd = wd_s.shape[1] // N_WGROUPS

    def group_copies(e, g):
        b = g % N_WBUF
        a0 = pl.multiple_of(g * ra, ra)
        d0 = pl.multiple_of(g * rd, rd)
        return (pltpu.make_async_copy(wg_hbm.at[l, e, pl.ds(a0, ra)], stg_g.at[b], sem.at[0, b]),
                pltpu.make_async_copy(wu_hbm.at[l, e, pl.ds(a0, ra)], stg_u.at[b], sem.at[1, b]),
                pltpu.make_async_copy(wd_hbm.at[l, e, pl.ds(d0, rd)], stg_d.at[b], sem.at[2, b]))

    def start_group(e, g):
        for cp in group_copies(e, g):
            cp.start()

    def begin_load(e):
        for g in range(N_WBUF):
            start_group(e, g)

    def cast_groups(e, slot, first_g, n_g):
        def body(it, carry):
            g = first_g + it
            b = g % N_WBUF
            for cp in group_copies(e, g):
                cp.wait()
            a0 = pl.multiple_of(g * ra, ra)
            d0 = pl.multiple_of(g * rd, rd)
            wg_s[slot, pl.ds(a0, ra), :] = stg_g[b].astype(BF16)
            wu_s[slot, pl.ds(a0, ra), :] = stg_u[b].astype(BF16)
            wd_s[slot, pl.ds(d0, rd), :] = stg_d[b].astype(BF16)

            @pl.when(g + N_WBUF < N_WGROUPS)
            def _():
                start_group(e, g + N_WBUF)
            return carry
        lax.fori_loop(0, n_g, body, 0)

    @pl.when(used)
    def _():
        slot = slot_ref[i]

        @pl.when(i == 0)
        def _():
            begin_load(te_ref[0])
            cast_groups(te_ref[0], slot, 0, N_WGROUPS)

        nxt = nxt_ref[i]

        @pl.when(nxt >= 0)
        def _():
            @pl.when(first_ref[i] == 1)
            def _():
                begin_load(nxt)
            cast_groups(nxt, 1 - slot, g0_ref[i], kk_ref[i])

        lo, hi = _unpack_pairs(x_ref[...])
        x = jnp.concatenate([lo.astype(BF16), hi.astype(BF16)], axis=1)
        a = _dot(x, wg_s[slot])
        b = _dot(x, wu_s[slot])
        hid = (a * jax.nn.sigmoid(a) * b).astype(BF16)
        y_ref[...] = _pack_pairs(_dot(hid, wd_s[slot]))

    @pl.when(jnp.logical_not(used))
    def _():
        y_ref[...] = jnp.zeros_like(y_ref)


def _gmm(tile_expert, n_used, plan, x_pad, w_gate, w_up, w_down, l, tm):
    rows, c = x_pad.shape
    _, _, d, de = w_gate.shape

    def row_map(i, te, nu, *_):
        return (jnp.minimum(i, jnp.maximum(nu[0] - 1, 0)), 0)

    grid_spec = pltpu.PrefetchScalarGridSpec(
        num_scalar_prefetch=2 + len(plan),
        grid=(rows // tm,),
        in_specs=[
            pl.BlockSpec((tm, c), row_map),
            pl.BlockSpec(memory_space=pl.ANY),
            pl.BlockSpec(memory_space=pl.ANY),
            pl.BlockSpec(memory_space=pl.ANY),
        ],
        out_specs=pl.BlockSpec((tm, c), lambda i, *_: (i, 0)),
        scratch_shapes=[
            pltpu.VMEM((2, d, de), BF16),
            pltpu.VMEM((2, d, de), BF16),
            pltpu.VMEM((2, de, d), BF16),
            pltpu.VMEM((N_WBUF, d // N_WGROUPS, de), F32),
            pltpu.VMEM((N_WBUF, d // N_WGROUPS, de), F32),
            pltpu.VMEM((N_WBUF, de // N_WGROUPS, d), F32),
            pltpu.SemaphoreType.DMA((3, N_WBUF)),
        ],
    )
    return pl.pallas_call(
        functools.partial(_gmm_kernel, l=l),
        grid_spec=grid_spec,
        out_shape=jax.ShapeDtypeStruct((rows, c), jnp.uint32),
        compiler_params=_params("arbitrary"),
        name="gmm",
    )(tile_expert, n_used, *plan, x_pad, w_gate, w_up, w_down)


def _combine_kernel(dest_ref, x_ref, r_ref, g_ref, yp_hbm, o_ref, ybuf, sem, *, tm, n, final_norm):
    j = pl.program_id(0)
    n_tiles = n // tm
    c = ybuf.shape[-1]

    def row_copy(src_row, slot, r, k):
        return pltpu.make_async_copy(yp_hbm.at[pl.ds(src_row, 1)], ybuf.at[slot, k, pl.ds(r, 1)], sem.at[slot])

    def start_gather(slot):
        for r in range(tm):
            for k in range(2):
                row_copy(dest_ref[k * n + j * tm + r], slot, r, k).start(priority=(2 * r + k) % ROW_DMA_QUEUES)

    def finish_tile(slot):
        for r in range(tm):
            for k in range(2):
                row_copy(0, slot, r, k).wait()
        g1 = r_ref[:, 2:3]
        g2 = r_ref[:, 3:4]
        lo1, hi1 = _unpack_pairs(ybuf[slot, 0])
        lo2, hi2 = _unpack_pairs(ybuf[slot, 1])
        x = x_ref[...]
        out = jnp.concatenate([x[:, :c] + g1 * lo1 + g2 * lo2, x[:, c:] + g1 * hi1 + g2 * hi2], axis=1)
        o_ref[...] = _rms(out, g_ref[...]) if final_norm else out

    for slot in range(2):
        pl.when((j < n_tiles) & (j % 2 == slot))(functools.partial(start_gather, slot))
    for slot in range(2):
        pl.when((j > 0) & ((j - 1) % 2 == slot))(functools.partial(finish_tile, slot))


def _combine(dest_flat, x, route, g, y_pad, tm, final_norm):
    n, d = x.shape
    c = y_pad.shape[1]
    grid_spec = pltpu.PrefetchScalarGridSpec(
        num_scalar_prefetch=1,
        grid=(n // tm + 1,),
        in_specs=[
            pl.BlockSpec((tm, d), lambda j, *_: (jnp.maximum(j - 1, 0), 0)),
            pl.BlockSpec((tm, ROUTE_LANES), lambda j, *_: (jnp.maximum(j - 1, 0), 0)),
            pl.BlockSpec((1, d), lambda j, *_: (0, 0)),
            pl.BlockSpec(memory_space=pl.ANY),
        ],
        out_specs=pl.BlockSpec((tm, d), lambda j, *_: (jnp.maximum(j - 1, 0), 0)),
        scratch_shapes=[pltpu.VMEM((2, 2, tm, c), jnp.uint32), pltpu.SemaphoreType.DMA((2,))],
    )
    return pl.pallas_call(
        functools.partial(_combine_kernel, tm=tm, n=n, final_norm=final_norm),
        grid_spec=grid_spec,
        out_shape=jax.ShapeDtypeStruct((n, d), F32),
        compiler_params=_params("arbitrary"),
        name="combine",
    )(dest_flat, x, route, g, y_pad)


def _tile(n, want):
    return min(n, want)


def _row_tiles(seq):
    return dict(
        mixer_in=_tile(seq, 1024),
        mixer_mid=_tile(seq, 256),
        xattn=_tile(seq, 512),
        route=_tile(seq, 1024),
        dispatch=_tile(seq, 1024),
        gmm=256,
        combine=_tile(seq, 256),
    )


MIXER_IN_COLS = 2048
KV_COLS = 1024


def kernel(x, mem, w_in, pool_mix, pool_scale, w_pool_out, conv_w, w_conv_out, w_mix_out, g_mix, g_xattn, g_mem, w_q, w_k, w_v, w_o, g_ffn, w_route_group, b_route_group, w_route_expert, b_route_expert, w_gate, w_up, w_down, g_final):
    bsz, seq, d = x.shape
    n_mem = mem.shape[1]
    depth = w_in.shape[0]
    n = bsz * seq
    xf = x.reshape(n, d)
    memf = mem.reshape(bsz * n_mem, d)

    tm = _row_tiles(seq)

    bf = lambda w: w.astype(BF16)
    w_in, pool_mix, w_pool_out, w_conv_out, w_mix_out = map(bf, (w_in, pool_mix, w_pool_out, w_conv_out, w_mix_out))
    w_q, w_k, w_v, w_o = map(bf, (w_q, w_k, w_v, w_o))
    pool_scale = pool_scale.reshape(depth, 1, -1)

    pad_lanes = ROUTE_LANES - N_EXPERTS - N_GROUPS
    for l in range(depth):
        row = lambda v: v[l].reshape(1, -1)
        z = _mixer_in(xf, row(g_mix), w_in, l, tm["mixer_in"], _tile(w_in.shape[2], MIXER_IN_COLS))
        xf = _mixer_mid(xf, z, pool_mix, pool_scale, w_pool_out, conv_w, w_conv_out, w_mix_out, l, seq,
                        tm["mixer_mid"])

        kv = _kv_proj(memf, row(g_mem), w_k, w_v, l, _tile(d, KV_COLS))
        w_r = jnp.concatenate([w_route_expert[l], w_route_group[l], jnp.zeros((d, pad_lanes), F32)], axis=1)
        w_r_hi = w_r.astype(BF16)
        w_r_lo = (w_r - w_r_hi.astype(F32)).astype(BF16)
        w_r3 = jnp.concatenate([w_r_hi, w_r_hi, w_r_lo], axis=0)
        b_r = jnp.concatenate([b_route_expert[l], b_route_group[l], jnp.zeros((pad_lanes,), F32)]).reshape(1, -1)
        xf = _xattn(xf, row(g_xattn), w_q, kv, w_o, l, seq, n_mem, tm["xattn"])
        hp, route, rows, cnt = _route(xf, row(g_ffn), w_r3, b_r, tm["route"])
        dest_flat, counts, pends, te, n_used = _moe_plan(rows, cnt, tm["gmm"])
        x_pad = _dispatch(dest_flat, counts, pends, hp, tm["dispatch"], tm["gmm"])
        plan = _gmm_plan(te, n_used, counts, pends, tm["gmm"])
        y_pad = _gmm(te, n_used, plan, x_pad, w_gate, w_up, w_down, l, tm["gmm"])
        xf = _combine(dest_flat, xf, route, g_final.reshape(1, -1), y_pad, tm["combine"],
                      final_norm=(l == depth - 1))
    return xf.reshape(bsz, seq, d)
```

```python
import functools
import math

import jax
import jax.numpy as jnp
from jax import lax
from jax.experimental import pallas as pl
from jax.experimental.pallas import tpu as pltpu

EPS = 1e-6
POOL_WINDOWS = (2, 4, 8, 16)
CONV_K = 3
XA_HEADS = 4
N_GROUPS = 4
EXPERTS_PER_GROUP = 8
N_EXPERTS = N_GROUPS * EXPERTS_PER_GROUP
HALO = 16
ROW_DMA_QUEUES = 2
ROUTE_LANES = 128
VMEM_LIMIT = 56 * 1024 * 1024

F32 = jnp.float32
BF16 = jnp.bfloat16


def _params(*sem):
    return pltpu.CompilerParams(dimension_semantics=sem, vmem_limit_bytes=VMEM_LIMIT)


def _resident(stack, l):
    nd = stack.ndim - 1
    return pl.BlockSpec((None,) + stack.shape[1:], lambda *_: (l,) + (0,) * nd, pipeline_mode=pl.Buffered(1))


def _rms(xf, g):
    ms = jnp.mean(xf * xf, axis=-1, keepdims=True)
    return xf * lax.rsqrt(ms + EPS) * g


def _dot(a, b):
    return jnp.dot(a, b, preferred_element_type=F32)


def _mixer_in_kernel(x_ref, g_ref, w_ref, z_ref, h_scr):
    @pl.when(pl.program_id(1) == 0)
    def _():
        h_scr[...] = _rms(x_ref[...], g_ref[...]).astype(BF16)

    z_ref[...] = _dot(h_scr[...], w_ref[...]).astype(z_ref.dtype)


def _mixer_in(x, g, w, l, tm, tn):
    n, d = x.shape
    c = w.shape[2]
    return pl.pallas_call(
        _mixer_in_kernel,
        grid=(n // tm, c // tn),
        in_specs=[
            pl.BlockSpec((tm, d), lambda i, j: (i, 0)),
            pl.BlockSpec((1, d), lambda i, j: (0, 0)),
            pl.BlockSpec((None, d, tn), lambda i, j: (l, 0, j)),
        ],
        out_specs=pl.BlockSpec((tm, tn), lambda i, j: (i, j)),
        out_shape=jax.ShapeDtypeStruct((n, c), BF16),
        scratch_shapes=[pltpu.VMEM((tm, d), BF16)],
        compiler_params=_params("arbitrary", "arbitrary"),
        name="mixer_in",
    )(x, g, w)


def _mixer_mid_kernel(x_ref, z_ref, zh_ref, pm_ref, ps_ref, wpo_ref, cw_ref, wco_ref, wmo_ref, o_ref,
                      *, tiles_per_seq, tm, d):
    pw = d // 2
    gw = pw // len(POOL_WINDOWS)
    t = pl.program_id(0) % tiles_per_seq
    keep = jnp.where(t == 0, 0.0, 1.0).astype(F32)
    pos = lax.broadcasted_iota(jnp.int32, (tm, gw), 0) + (t * tm + 1)

    uc = z_ref[:, pw:2 * pw].astype(F32)
    bg = z_ref[:, 2 * pw:3 * pw].astype(F32)
    cg = z_ref[:, 3 * pw:4 * pw].astype(F32)
    v = cg * uc
    vh = zh_ref[:, 3 * pw:4 * pw].astype(F32) * zh_ref[:, pw:2 * pw].astype(F32) * keep
    vext = jnp.concatenate([vh, v], axis=0)
    y = cw_ref[CONV_K - 1:CONV_K, :] * v
    for j in range(CONV_K - 1):
        y = y + cw_ref[j:j + 1, :] * pltpu.roll(vext, CONV_K - 1 - j, axis=0)[HALO:, :]
    br_b = _dot((bg * y).astype(BF16), wco_ref[...])
    gated_b = jax.nn.sigmoid(z_ref[:, 3 * d:4 * d].astype(F32)) * br_b

    parts = []
    for gi, w in enumerate(POOL_WINDOWS):
        cols = slice(gi * gw, (gi + 1) * gw)
        u = z_ref[:, cols].astype(F32)
        ext = jnp.concatenate([zh_ref[:, cols].astype(F32) * keep, u], axis=0)
        s = 1
        while s < w:
            ext = ext + pltpu.roll(ext, s, axis=0)
            s *= 2
        cnt = jnp.minimum(pos, w).astype(F32)
        p = (ext[HALO:, :] / cnt - u).astype(BF16)
        m = _dot(p, pm_ref[gi]) * ps_ref[:, cols]
        parts.append(m.astype(BF16))
    br_a = _dot(jnp.concatenate(parts, axis=1), wpo_ref[...])

    merged = jax.nn.sigmoid(z_ref[:, 2 * d:3 * d].astype(F32)) * br_a + gated_b
    o_ref[...] = x_ref[...] + _dot(merged.astype(BF16), wmo_ref[...])


def _mixer_mid(x, z, pool_mix, pool_scale, w_pool_out, conv_w, w_conv_out, w_mix_out, l, seq, tm):
    n, d = x.shape
    c = z.shape[1]
    hb = tm // HALO
    kern = functools.partial(_mixer_mid_kernel, tiles_per_seq=seq // tm, tm=tm, d=d)
    return pl.pallas_call(
        kern,
        grid=(n // tm,),
        in_specs=[
            pl.BlockSpec((tm, d), lambda i: (i, 0)),
            pl.BlockSpec((tm, c), lambda i: (i, 0)),
            pl.BlockSpec((HALO, c // 2), lambda i: (jnp.maximum(i * hb - 1, 0), 0)),
            _resident(pool_mix, l),
            _resident(pool_scale, l),
            _resident(w_pool_out, l),
            _resident(conv_w, l),
            _resident(w_conv_out, l),
            _resident(w_mix_out, l),
        ],
        out_specs=pl.BlockSpec((tm, d), lambda i: (i, 0)),
        out_shape=jax.ShapeDtypeStruct((n, d), F32),
        compiler_params=_params("arbitrary"),
        name="mixer_mid",
    )(x, z, z, pool_mix, pool_scale, w_pool_out, conv_w, w_conv_out, w_mix_out)


def _kv_kernel(m_ref, g_ref, wk_ref, wv_ref, o_ref, *, nb):
    mn = _rms(m_ref[...], g_ref[...]).astype(BF16)
    j = pl.program_id(0)

    @pl.when(j < nb)
    def _():
        o_ref[...] = _dot(mn, wk_ref[...]).astype(o_ref.dtype)

    @pl.when(j >= nb)
    def _():
        o_ref[...] = _dot(mn, wv_ref[...]).astype(o_ref.dtype)


def _kv_proj(mem, g, w_k, w_v, l, tn):
    r, d = mem.shape
    nb = w_k.shape[2] // tn
    return pl.pallas_call(
        functools.partial(_kv_kernel, nb=nb),
        grid=(2 * nb,),
        in_specs=[
            pl.BlockSpec((r, d), lambda j: (0, 0)),
            pl.BlockSpec((1, d), lambda j: (0, 0)),
            pl.BlockSpec((None, d, tn), lambda j: (l, 0, jnp.minimum(j, nb - 1))),
            pl.BlockSpec((None, d, tn), lambda j: (l, 0, jnp.maximum(j - nb, 0))),
        ],
        out_specs=pl.BlockSpec((r, tn), lambda j: (0, j)),
        out_shape=jax.ShapeDtypeStruct((r, 2 * nb * tn), BF16),
        compiler_params=_params("arbitrary"),
        name="kv_proj",
    )(mem, g, w_k, w_v)


def _attend_rows(x, g, wq_ref, kv_ref, wo_ref, d):
    hd = d // XA_HEADS
    scale = 1.0 / math.sqrt(hd)
    q = _dot(_rms(x, g).astype(BF16), wq_ref[...]).astype(BF16)
    heads = []
    for h in range(XA_HEADS):
        qh = q[:, h * hd:(h + 1) * hd]
        kh = kv_ref[:, h * hd:(h + 1) * hd]
        vh = kv_ref[:, d + h * hd:d + (h + 1) * hd]
        s = lax.dot_general(qh, kh, (((1,), (1,)), ((), ())), preferred_element_type=F32) * scale
        e = jnp.exp(s - jnp.max(s, axis=-1, keepdims=True))
        p = (e / jnp.sum(e, axis=-1, keepdims=True)).astype(BF16)
        heads.append(_dot(p, vh).astype(BF16))
    return x + _dot(jnp.concatenate(heads, axis=1), wo_ref[...])


def _xattn_kernel(x_ref, g_ref, wq_ref, kv_ref, wo_ref, o_ref, *, d):
    o_ref[...] = _attend_rows(x_ref[...], g_ref[...], wq_ref, kv_ref, wo_ref, d)


def _xattn(x, g, w_q, kv, w_o, l, seq, n_mem, tm):
    n, d = x.shape
    tiles_per_seq = seq // tm
    return pl.pallas_call(
        functools.partial(_xattn_kernel, d=d),
        grid=(n // tm,),
        in_specs=[
            pl.BlockSpec((tm, d), lambda i: (i, 0)),
            pl.BlockSpec((1, d), lambda i: (0, 0)),
            _resident(w_q, l),
            pl.BlockSpec((n_mem, 2 * d), lambda i: (i // tiles_per_seq, 0)),
            _resident(w_o, l),
        ],
        out_specs=pl.BlockSpec((tm, d), lambda i: (i, 0)),
        out_shape=jax.ShapeDtypeStruct((n, d), F32),
        compiler_params=_params("arbitrary"),
        name="xattn",
    )(x, g, w_q, kv, w_o)


def _pack_pairs(v):
    c = v.shape[1] // 2
    bits = lax.bitcast_convert_type(v.astype(BF16).astype(F32), jnp.uint32)
    return (bits[:, :c] >> 16) | bits[:, c:]


def _unpack_pairs(w):
    lo = lax.bitcast_convert_type(w << 16, F32)
    hi = lax.bitcast_convert_type(w & jnp.uint32(0xFFFF0000), F32)
    return lo, hi


ROW_E1, ROW_E2, ROW_R1_HI, ROW_R1_LO, ROW_R2_HI, ROW_R2_LO = range(6)
ROUTE_ROWS = 8
RANK_RADIX = 256


def _route_rows(x, g, wr, br, carry):
    tm = x.shape[0]
    h = _rms(x, g)
    h_hi = h.astype(BF16)
    h_lo = (h - h_hi.astype(F32)).astype(BF16)
    logits = _dot(jnp.concatenate([h_hi, h_lo, h_hi], axis=1), wr) + br
    lane = lax.broadcasted_iota(jnp.int32, logits.shape, 1)
    neg = jnp.float32(-jnp.inf)
    big = jnp.int32(ROUTE_LANES)

    is_g = (lane >= N_EXPERTS) & (lane < N_EXPERTS + N_GROUPS)
    lg = jnp.where(is_g, logits, neg)
    gmax = jnp.max(lg, axis=-1, keepdims=True)
    g_sel = jnp.min(jnp.where(lg == gmax, lane - N_EXPERTS, big), axis=-1, keepdims=True)
    p_group = 1.0 / jnp.sum(jnp.exp(lg - gmax), axis=-1, keepdims=True)

    in_group = (lane >= g_sel * EXPERTS_PER_GROUP) & (lane < (g_sel + 1) * EXPERTS_PER_GROUP)
    le = jnp.where(in_group, logits, neg)
    v1 = jnp.max(le, axis=-1, keepdims=True)
    i1 = jnp.min(jnp.where(le == v1, lane, big), axis=-1, keepdims=True)
    le2 = jnp.where(lane == i1, neg, le)
    v2 = jnp.max(le2, axis=-1, keepdims=True)
    i2 = jnp.min(jnp.where(le2 == v2, lane, big), axis=-1, keepdims=True)
    t = jnp.exp(v2 - v1)
    g1 = p_group / (1.0 + t)
    g2 = p_group * t / (1.0 + t)

    oh1 = lane == i1
    oh2 = lane == i2
    both = jnp.where(oh1, 1.0, 0.0) + jnp.where(oh2, 1.0, 0.0)
    rr = lax.broadcasted_iota(jnp.int32, (tm, tm), 0)
    cc = lax.broadcasted_iota(jnp.int32, (tm, tm), 1)
    lower = jnp.where(cc < rr, 1.0, 0.0).astype(BF16)
    before = _dot(lower, both.astype(BF16)) + carry
    rank1 = jnp.sum(jnp.where(oh1, before, 0.0), axis=-1, keepdims=True)
    rank2 = jnp.sum(jnp.where(oh2, before, 0.0), axis=-1, keepdims=True)

    gates = jnp.where(lane == 2, g1, 0.0)
    gates = jnp.where(lane == 3, g2, gates)

    r1_hi = jnp.floor(rank1 * (1.0 / RANK_RADIX))
    r2_hi = jnp.floor(rank2 * (1.0 / RANK_RADIX))
    vals = jnp.where(lane == ROW_E1, i1.astype(F32), 0.0)
    vals = jnp.where(lane == ROW_E2, i2.astype(F32), vals)
    vals = jnp.where(lane == ROW_R1_HI, r1_hi, vals)
    vals = jnp.where(lane == ROW_R1_LO, rank1 - r1_hi * RANK_RADIX, vals)
    vals = jnp.where(lane == ROW_R2_HI, r2_hi, vals)
    vals = jnp.where(lane == ROW_R2_LO, rank2 - r2_hi * RANK_RADIX, vals)
    pick = jnp.where(lax.broadcasted_iota(jnp.int32, (ROUTE_ROWS, ROUTE_LANES), 0)
                     == lax.broadcasted_iota(jnp.int32, (ROUTE_ROWS, ROUTE_LANES), 1), 1.0, 0.0).astype(BF16)
    rows = lax.dot_general(pick, vals.astype(BF16), (((1,), (1,)), ((), ())), preferred_element_type=F32)
    return _pack_pairs(h), gates, rows, carry + jnp.sum(both, axis=0, keepdims=True)


def _route_kernel(x_ref, g_ref, wr_ref, br_ref, hp_ref, r_ref, rows_ref, cnt_ref, carry_scr):
    @pl.when(pl.program_id(0) == 0)
    def _():
        carry_scr[...] = jnp.zeros_like(carry_scr)

    hp, gates, row_form, carry = _route_rows(x_ref[...], g_ref[...], wr_ref[...], br_ref[...], carry_scr[...])
    hp_ref[...] = hp
    r_ref[...] = gates
    rows_ref[...] = row_form
    carry_scr[...] = carry
    cnt_ref[...] = carry


def _route(x, g, w_r, b_r, tm):
    n, d = x.shape
    return pl.pallas_call(
        _route_kernel,
        grid=(n // tm,),
        in_specs=[
            pl.BlockSpec((tm, d), lambda i: (i, 0)),
            pl.BlockSpec((1, d), lambda i: (0, 0)),
            pl.BlockSpec(w_r.shape, lambda i: (0, 0)),
            pl.BlockSpec((1, ROUTE_LANES), lambda i: (0, 0)),
        ],
        out_specs=[
            pl.BlockSpec((tm, d // 2), lambda i: (i, 0)),
            pl.BlockSpec((tm, ROUTE_LANES), lambda i: (i, 0)),
            pl.BlockSpec((ROUTE_ROWS, tm), lambda i: (0, i)),
            pl.BlockSpec((1, ROUTE_LANES), lambda i: (0, 0)),
        ],
        out_shape=[
            jax.ShapeDtypeStruct((n, d // 2), jnp.uint32),
            jax.ShapeDtypeStruct((n, ROUTE_LANES), F32),
            jax.ShapeDtypeStruct((ROUTE_ROWS, n), F32),
            jax.ShapeDtypeStruct((1, ROUTE_LANES), F32),
        ],
        scratch_shapes=[pltpu.VMEM((1, ROUTE_LANES), F32)],
        compiler_params=_params("arbitrary"),
        name="route",
    )(x, g, w_r, b_r)


def _moe_plan(rows, cnt, tm):
    n = rows.shape[1]
    ri = rows.astype(jnp.int32)
    experts = jnp.arange(N_EXPERTS, dtype=jnp.int32)
    counts = cnt[0, :N_EXPERTS].astype(jnp.int32)
    padded = (counts + tm - 1) // tm * tm
    pends = jnp.sum(jnp.where(experts[None, :] <= experts[:, None], padded[None, :], 0), axis=1)
    pstarts = pends - padded

    def dest(e, hi, lo):
        start = jnp.sum(jnp.where(e[:, None] == experts[None, :], pstarts[None, :], 0), axis=1)
        return start + hi * RANK_RADIX + lo

    dest_flat = jnp.concatenate([dest(ri[ROW_E1], ri[ROW_R1_HI], ri[ROW_R1_LO]),
                                 dest(ri[ROW_E2], ri[ROW_R2_HI], ri[ROW_R2_LO])])
    n_tiles = (2 * n + N_EXPERTS * tm) // tm
    n_used = (pends[-1] // tm).astype(jnp.int32)
    tile_start = jnp.arange(n_tiles, dtype=jnp.int32) * tm
    te = jnp.sum((pends[None, :] <= tile_start[:, None]).astype(jnp.int32), axis=1)
    last_used = jnp.max(jnp.where(counts > 0, experts, 0))
    te = jnp.minimum(te, last_used)
    return dest_flat, counts, pends, te, n_used.reshape(1)


def _dispatch_kernel(dest_ref, cnt_ref, pend_ref, h_ref, xp_hbm, zero_scr, sem, zsem, *, tm, tile, n):
    i = pl.program_id(0)

    @pl.when(i == 0)
    def _():
        zero_scr[...] = jnp.zeros_like(zero_scr)

        def zero_tile(t, carry):
            cp = pltpu.make_async_copy(zero_scr, xp_hbm.at[pl.ds(pl.multiple_of(t * tile, tile), tile)], zsem)
            cp.start()
            cp.wait()
            return carry

        lax.fori_loop(pend_ref[N_EXPERTS - 1] // tile, xp_hbm.shape[0] // tile, zero_tile, 0)
        for e in range(N_EXPERTS):
            @pl.when(cnt_ref[e] > 0)
            def _():
                start = pl.multiple_of(pend_ref[e] - tile, tile)
                pltpu.make_async_copy(zero_scr, xp_hbm.at[pl.ds(start, tile)], zsem).start()
        for e in range(N_EXPERTS):
            @pl.when(cnt_ref[e] > 0)
            def _():
                start = pl.multiple_of(pend_ref[e] - tile, tile)
                pltpu.make_async_copy(zero_scr, xp_hbm.at[pl.ds(start, tile)], zsem).wait()

    base = i * tm

    def row_copy(r, k):
        return pltpu.make_async_copy(h_ref.at[pl.ds(r, 1)], xp_hbm.at[pl.ds(dest_ref[k * n + base + r], 1)], sem)

    copies = [row_copy(r, k) for r in range(tm) for k in range(2)]
    for ci, cp in enumerate(copies):
        cp.start(priority=ci % ROW_DMA_QUEUES)
    for cp in copies:
        cp.wait()


def _dispatch(dest_flat, counts, pends, hp, tm, tile):
    n, c = hp.shape
    rows = 2 * n + N_EXPERTS * tile
    grid_spec = pltpu.PrefetchScalarGridSpec(
        num_scalar_prefetch=3,
        grid=(n // tm,),
        in_specs=[pl.BlockSpec((tm, c), lambda i, *_: (i, 0))],
        out_specs=pl.BlockSpec(memory_space=pl.ANY),
        scratch_shapes=[pltpu.VMEM((tile, c), jnp.uint32), pltpu.SemaphoreType.DMA, pltpu.SemaphoreType.DMA],
    )
    return pl.pallas_call(
        functools.partial(_dispatch_kernel, tm=tm, tile=tile, n=n),
        grid_spec=grid_spec,
        out_shape=jax.ShapeDtypeStruct((rows, c), jnp.uint32),
        compiler_params=_params("arbitrary"),
        name="dispatch",
    )(dest_flat, counts, pends, hp)


N_WGROUPS = 8
N_WBUF = 2


def _gmm_plan(te, n_used, counts, pends, tm):
    n_tiles = te.shape[0]
    idx = jnp.arange(n_tiles, dtype=jnp.int32)
    experts = jnp.arange(N_EXPERTS, dtype=jnp.int32)
    by_tile = lambda table: jnp.sum(jnp.where(te[:, None] == experts[None, :], table[None, :], 0), axis=1)
    run_len = by_tile((counts + tm - 1) // tm)
    pos = idx - by_tile(pends // tm - (counts + tm - 1) // tm)
    later = (experts[None, :] > experts[:, None]) & (counts[None, :] > 0)
    nxt_e = jnp.min(jnp.where(later, experts[None, :], N_EXPERTS), axis=1)
    nxt = by_tile(jnp.where(nxt_e < N_EXPERTS, nxt_e, -1))
    used_before = jnp.sum(jnp.where((experts[None, :] < experts[:, None]) & (counts[None, :] > 0), 1, 0), axis=1)
    slot = by_tile(used_before % 2)
    safe_len = jnp.maximum(run_len, 1)
    g0 = pos * N_WGROUPS // safe_len
    kk = (pos + 1) * N_WGROUPS // safe_len - g0
    first = (pos == 0).astype(jnp.int32)
    fix = lambda v: jnp.where(idx < n_used, v, 0).astype(jnp.int32)
    return fix(nxt) - (idx >= n_used).astype(jnp.int32), fix(slot), fix(first), fix(g0), fix(kk)


def _gmm_kernel(te_ref, nu_ref, nxt_ref, slot_ref, first_ref, g0_ref, kk_ref, x_ref, wg_hbm, wu_hbm, wd_hbm, y_ref,
                wg_s, wu_s, wd_s, stg_g, stg_u, stg_d, sem, *, l):
    i = pl.program_id(0)
    used = i < nu_ref[0]
    ra = wg_s.shape[1] // N_WGROUPS
    rd = wd_s.shape[1] // N_WGROUPS

    def group_copies(e, g):
        b = g % N_WBUF
        a0 = pl.multiple_of(g * ra, ra)
        d0 = pl.multiple_of(g * rd, rd)
        return (pltpu.make_async_copy(wg_hbm.at[l, e, pl.ds(a0, ra)], stg_g.at[b], sem.at[0, b]),
                pltpu.make_async_copy(wu_hbm.at[l, e, pl.ds(a0, ra)], stg_u.at[b], sem.at[1, b]),
                pltpu.make_async_copy(wd_hbm.at[l, e, pl.ds(d0, rd)], stg_d.at[b], sem.at[2, b]))

    def start_group(e, g):
        for cp in group_copies(e, g):
            cp.start()

    def begin_load(e):
        for g in range(N_WBUF):
            start_group(e, g)

    def cast_groups(e, slot, first_g, n_g):
        def body(it, carry):
            g = first_g + it
            b = g % N_WBUF
            for cp in group_copies(e, g):
                cp.wait()
            a0 = pl.multiple_of(g * ra, ra)
            d0 = pl.multiple_of(g * rd, rd)
            wg_s[slot, pl.ds(a0, ra), :] = stg_g[b].astype(BF16)
            wu_s[slot, pl.ds(a0, ra), :] = stg_u[b].astype(BF16)
            wd_s[slot, pl.ds(d0, rd), :] = stg_d[b].astype(BF16)

            @pl.when(g + N_WBUF < N_WGROUPS)
            def _():
                start_group(e, g + N_WBUF)
            return carry
        lax.fori_loop(0, n_g, body, 0)

    @pl.when(used)
    def _():
        slot = slot_ref[i]

        @pl.when(i == 0)
        def _():
            begin_load(te_ref[0])
            cast_groups(te_ref[0], slot, 0, N_WGROUPS)

        nxt = nxt_ref[i]

        @pl.when(nxt >= 0)
        def _():
            @pl.when(first_ref[i] == 1)
            def _():
                begin_load(nxt)
            cast_groups(nxt, 1 - slot, g0_ref[i], kk_ref[i])

        lo, hi = _unpack_pairs(x_ref[...])
        x = jnp.concatenate([lo.astype(BF16), hi.astype(BF16)], axis=1)
        a = _dot(x, wg_s[slot])
        b = _dot(x, wu_s[slot])
        hid = (a * jax.nn.sigmoid(a) * b).astype(BF16)
        y_ref[...] = _pack_pairs(_dot(hid, wd_s[slot]))

    @pl.when(jnp.logical_not(used))
    def _():
        y_ref[...] = jnp.zeros_like(y_ref)


def _gmm(tile_expert, n_used, plan, x_pad, w_gate, w_up, w_down, l, tm):
    rows, c = x_pad.shape
    _, _, d, de = w_gate.shape

    def row_map(i, te, nu, *_):
        return (jnp.minimum(i, jnp.maximum(nu[0] - 1, 0)), 0)

    grid_spec = pltpu.PrefetchScalarGridSpec(
        num_scalar_prefetch=2 + len(plan),
        grid=(rows // tm,),
        in_specs=[
            pl.BlockSpec((tm, c), row_map),
            pl.BlockSpec(memory_space=pl.ANY),
            pl.BlockSpec(memory_space=pl.ANY),
            pl.BlockSpec(memory_space=pl.ANY),
        ],
        out_specs=pl.BlockSpec((tm, c), lambda i, *_: (i, 0)),
        scratch_shapes=[
            pltpu.VMEM((2, d, de), BF16),
            pltpu.VMEM((2, d, de), BF16),
            pltpu.VMEM((2, de, d), BF16),
            pltpu.VMEM((N_WBUF, d // N_WGROUPS, de), F32),
            pltpu.VMEM((N_WBUF, d // N_WGROUPS, de), F32),
            pltpu.VMEM((N_WBUF, de // N_WGROUPS, d), F32),
            pltpu.SemaphoreType.DMA((3, N_WBUF)),
        ],
    )
    return pl.pallas_call(
        functools.partial(_gmm_kernel, l=l),
        grid_spec=grid_spec,
        out_shape=jax.ShapeDtypeStruct((rows, c), jnp.uint32),
        compiler_params=_params("arbitrary"),
        name="gmm",
    )(tile_expert, n_used, *plan, x_pad, w_gate, w_up, w_down)


GATHER_RING = 3


def _combine_kernel(dest_ref, x_ref, r_ref, g_ref, yp_hbm, o_ref, ybuf, sem, *, tm, n, final_norm):
    j = pl.program_id(0)
    n_tiles = n // tm
    c = ybuf.shape[-1]

    def row_copy(src_row, slot, r, k):
        return pltpu.make_async_copy(yp_hbm.at[pl.ds(src_row, 1)], ybuf.at[slot, k, pl.ds(r, 1)], sem.at[slot])

    def start_gather(slot):
        for r in range(tm):
            for k in range(2):
                row_copy(dest_ref[k * n + j * tm + r], slot, r, k).start(priority=(2 * r + k) % ROW_DMA_QUEUES)

    def finish_tile(slot):
        for r in range(tm):
            for k in range(2):
                row_copy(0, slot, r, k).wait()
        g1 = r_ref[:, 2:3]
        g2 = r_ref[:, 3:4]
        lo1, hi1 = _unpack_pairs(ybuf[slot, 0])
        lo2, hi2 = _unpack_pairs(ybuf[slot, 1])
        x = x_ref[...]
        out = jnp.concatenate([x[:, :c] + g1 * lo1 + g2 * lo2, x[:, c:] + g1 * hi1 + g2 * hi2], axis=1)
        o_ref[...] = _rms(out, g_ref[...]) if final_norm else out

    for slot in range(GATHER_RING):
        pl.when((j < n_tiles) & (j % GATHER_RING == slot))(functools.partial(start_gather, slot))
    done = j - (GATHER_RING - 1)
    for slot in range(GATHER_RING):
        pl.when((done >= 0) & (done % GATHER_RING == slot))(functools.partial(finish_tile, slot))


def _combine(dest_flat, x, route, g, y_pad, tm, final_norm):
    n, d = x.shape
    c = y_pad.shape[1]
    lag = GATHER_RING - 1
    grid_spec = pltpu.PrefetchScalarGridSpec(
        num_scalar_prefetch=1,
        grid=(n // tm + lag,),
        in_specs=[
            pl.BlockSpec((tm, d), lambda j, *_: (jnp.maximum(j - lag, 0), 0)),
            pl.BlockSpec((tm, ROUTE_LANES), lambda j, *_: (jnp.maximum(j - lag, 0), 0)),
            pl.BlockSpec((1, d), lambda j, *_: (0, 0)),
            pl.BlockSpec(memory_space=pl.ANY),
        ],
        out_specs=pl.BlockSpec((tm, d), lambda j, *_: (jnp.maximum(j - lag, 0), 0)),
        scratch_shapes=[pltpu.VMEM((GATHER_RING, 2, tm, c), jnp.uint32), pltpu.SemaphoreType.DMA((GATHER_RING,))],
    )
    return pl.pallas_call(
        functools.partial(_combine_kernel, tm=tm, n=n, final_norm=final_norm),
        grid_spec=grid_spec,
        out_shape=jax.ShapeDtypeStruct((n, d), F32),
        compiler_params=_params("arbitrary"),
        name="combine",
    )(dest_flat, x, route, g, y_pad)


def _tile(n, want):
    return min(n, want)


def _row_tiles(seq):
    return dict(
        mixer_in=_tile(seq, 1024),
        mixer_mid=_tile(seq, 256),
        xattn=_tile(seq, 512),
        route=_tile(seq, 1024),
        dispatch=_tile(seq, 1024),
        gmm=256,
        combine=_tile(seq, 256),
    )


MIXER_IN_COLS = 2048
KV_COLS = 1024


def kernel(x, mem, w_in, pool_mix, pool_scale, w_pool_out, conv_w, w_conv_out, w_mix_out, g_mix, g_xattn, g_mem, w_q, w_k, w_v, w_o, g_ffn, w_route_group, b_route_group, w_route_expert, b_route_expert, w_gate, w_up, w_down, g_final):
    bsz, seq, d = x.shape
    n_mem = mem.shape[1]
    depth = w_in.shape[0]
    n = bsz * seq
    xf = x.reshape(n, d)
    memf = mem.reshape(bsz * n_mem, d)

    tm = _row_tiles(seq)

    bf = lambda w: w.astype(BF16)
    w_in, pool_mix, w_pool_out, w_conv_out, w_mix_out = map(bf, (w_in, pool_mix, w_pool_out, w_conv_out, w_mix_out))
    w_q, w_k, w_v, w_o = map(bf, (w_q, w_k, w_v, w_o))
    pool_scale = pool_scale.reshape(depth, 1, -1)

    pad_lanes = ROUTE_LANES - N_EXPERTS - N_GROUPS
    for l in range(depth):
        row = lambda v: v[l].reshape(1, -1)
        z = _mixer_in(xf, row(g_mix), w_in, l, tm["mixer_in"], _tile(w_in.shape[2], MIXER_IN_COLS))
        xf = _mixer_mid(xf, z, pool_mix, pool_scale, w_pool_out, conv_w, w_conv_out, w_mix_out, l, seq,
                        tm["mixer_mid"])

        kv = _kv_proj(memf, row(g_mem), w_k, w_v, l, _tile(d, KV_COLS))
        w_r = jnp.concatenate([w_route_expert[l], w_route_group[l], jnp.zeros((d, pad_lanes), F32)], axis=1)
        w_r_hi = w_r.astype(BF16)
        w_r_lo = (w_r - w_r_hi.astype(F32)).astype(BF16)
        w_r3 = jnp.concatenate([w_r_hi, w_r_hi, w_r_lo], axis=0)
        b_r = jnp.concatenate([b_route_expert[l], b_route_group[l], jnp.zeros((pad_lanes,), F32)]).reshape(1, -1)
        xf = _xattn(xf, row(g_xattn), w_q, kv, w_o, l, seq, n_mem, tm["xattn"])
        hp, route, rows, cnt = _route(xf, row(g_ffn), w_r3, b_r, tm["route"])
        dest_flat, counts, pends, te, n_used = _moe_plan(rows, cnt, tm["gmm"])
        x_pad = _dispatch(dest_flat, counts, pends, hp, tm["dispatch"], tm["gmm"])
        plan = _gmm_plan(te, n_used, counts, pends, tm["gmm"])
        y_pad = _gmm(te, n_used, plan, x_pad, w_gate, w_up, w_down, l, tm["gmm"])
        xf = _combine(dest_flat, xf, route, g_final.reshape(1, -1), y_pad, tm["combine"],
                      final_norm=(l == depth - 1))
    return xf.reshape(bsz, seq, d)
```
